```python
import jax, jax.numpy as jnp
from jax import lax
import numpy as np

D_MODEL = 1024
BATCH = 16
SEQ = 2048
DEPTH = 1

HGRN_EXPAND = 128
HGRN_WIDTH = D_MODEL
HGRN_HEADS = HGRN_WIDTH // HGRN_EXPAND
HGRN_HEAD_V = HGRN_WIDTH // HGRN_HEADS
HGRN_CHUNK = 16
POOL_WINDOWS = (2, 4, 8, 16)
POOL_GROUPS = len(POOL_WINDOWS)
POOL_WIDTH = D_MODEL
POOL_GROUP_DIM = POOL_WIDTH // POOL_GROUPS
D_FF = 4 * D_MODEL
IN_COLS = 4 * HGRN_WIDTH + POOL_WIDTH + 2 * D_MODEL
DEEPNORM_ALPHA = (2.0 * DEPTH) ** 0.25
DEEPNORM_BETA = (8.0 * DEPTH) ** -0.25
LN_EPS = 1e-5
RMS_EPS = 1e-6

kernel_name = "hgrn2_multiscale_pool_gated_hybrid_deepnorm"


def layer_norm(x, g, b):
    xf = x.astype(jnp.float32)
    mu = jnp.mean(xf, axis=-1, keepdims=True)
    var = jnp.mean(jnp.square(xf - mu), axis=-1, keepdims=True)
    y = (xf - mu) * lax.rsqrt(var + LN_EPS)
    return (y * g.astype(jnp.float32) + b.astype(jnp.float32)).astype(x.dtype)


def hgrn2_chunked(q, k, v, log_f):
    B, S, H, dk = q.shape
    dv = v.shape[-1]
    n = S // HGRN_CHUNK

    def to_chunks(t):
        return t.reshape(B, n, HGRN_CHUNK, H, t.shape[-1]).transpose(1, 0, 3, 2, 4)

    q, k, v, log_f = to_chunks(q), to_chunks(k), to_chunks(v), to_chunks(log_f)
    G = jnp.cumsum(log_f, axis=3)
    G_last = G[:, :, :, -1:, :]
    q_dec = q * jnp.exp(G)
    k_inv = k * jnp.exp(-G)
    k_to_end = k * jnp.exp(G_last - G)
    causal = jnp.tril(jnp.ones((HGRN_CHUNK, HGRN_CHUNK), dtype=bool))
    scores = jnp.einsum('nbhtd,nbhsd->nbhts', q_dec, k_inv)
    scores = jnp.where(causal, scores, 0.0)
    o_intra = jnp.einsum('nbhts,nbhsv->nbhtv', scores, v)

    def step(state, xs):
        q_c, k_c, v_c, decay_c = xs
        o_inter = jnp.einsum('bhtd,bhdv->bhtv', q_c, state)
        new_state = jnp.swapaxes(decay_c, -1, -2) * state + jnp.einsum('bhsd,bhsv->bhdv', k_c, v_c)
        return new_state, o_inter

    state0 = jnp.zeros((B, H, dk, dv), jnp.float32)
    _, o_inter = lax.scan(step, state0, (q_dec, k_to_end, v, jnp.exp(G_last)))
    o = o_intra + o_inter
    return o.transpose(1, 0, 3, 2, 4).reshape(B, S, H, dv)


def causal_multiscale_pool(v):
    B, S, _ = v.shape
    vg = v.reshape(B, S, POOL_GROUPS, POOL_GROUP_DIM).astype(jnp.float32)
    csum = jnp.cumsum(vg, axis=1)
    pos = jnp.arange(S)
    outs = []
    for g, w in enumerate(POOL_WINDOWS):
        c = csum[:, :, g]
        lagged = jnp.pad(c, ((0, 0), (w, 0), (0, 0)))[:, :S]
        count = jnp.minimum(pos + 1, w).astype(jnp.float32)[:, None]
        outs.append((c - lagged) / count - vg[:, :, g])
    return jnp.stack(outs, axis=2).astype(v.dtype)


def hybrid_mixer(x, w_in, lower_bound, hgrn_norm_g, w_a, w_pool, pool_scale, w_out):
    B, S, _ = x.shape
    proj = x @ w_in
    hw = HGRN_WIDTH
    splits = [hw, 2 * hw, 3 * hw, 4 * hw, 4 * hw + POOL_WIDTH, 4 * hw + POOL_WIDTH + D_MODEL]
    q, f_pre, i_val, o_gate, pool_v, gate_a, gate_b = jnp.split(proj, splits, axis=-1)

    qf = jax.nn.silu(q.astype(jnp.float32)) * (HGRN_EXPAND ** -0.5)
    lb = lower_bound.astype(jnp.float32)
    f = lb + (1.0 - lb) * jax.nn.sigmoid(f_pre.astype(jnp.float32))
    k = 1.0 - f
    log_f = jnp.log(f)
    shp = (B, S, HGRN_HEADS, HGRN_EXPAND)
    o = hgrn2_chunked(qf.reshape(shp), k.reshape(shp),
                      i_val.astype(jnp.float32).reshape(B, S, HGRN_HEADS, HGRN_HEAD_V), log_f.reshape(shp))
    o = o * lax.rsqrt(jnp.mean(jnp.square(o), axis=-1, keepdims=True) + RMS_EPS)
    o = o.reshape(B, S, hw) * hgrn_norm_g.astype(jnp.float32) * jax.nn.sigmoid(o_gate.astype(jnp.float32))
    a = o.astype(x.dtype) @ w_a

    pooled = causal_multiscale_pool(pool_v)
    b = jnp.einsum('bsgc,gcd->bsgd', pooled, w_pool).reshape(B, S, POOL_WIDTH) * pool_scale

    merged = jax.nn.sigmoid(gate_a) * a + jax.nn.sigmoid(gate_b) * b
    return merged @ w_out


def sq_relu_mlp(x, w_up, w_down):
    return jnp.square(jax.nn.relu(x @ w_up)) @ w_down


def setup_inputs(seed: int = 0) -> dict:
    key = jax.random.key(seed)
    ks = jax.random.split(key, 15)
    f32 = jnp.float32
    nrm = lambda k, s: jax.random.normal(k, s, f32)
    return {
        "x": nrm(ks[0], (BATCH, SEQ, D_MODEL)),
        "w_in": nrm(ks[1], (DEPTH, D_MODEL, IN_COLS)) * D_MODEL ** -0.5,
        "lb_logits": nrm(ks[2], (DEPTH + 1, HGRN_WIDTH)) * 0.5,
        "hgrn_norm_g": 1.0 + 0.05 * nrm(ks[3], (DEPTH, HGRN_WIDTH)),
        "w_a": nrm(ks[4], (DEPTH, HGRN_WIDTH, D_MODEL)) * HGRN_WIDTH ** -0.5,
        "w_pool": nrm(ks[5], (DEPTH, POOL_GROUPS, POOL_GROUP_DIM, POOL_GROUP_DIM)) * POOL_GROUP_DIM ** -0.5,
        "pool_scale": 1.0 + 0.05 * nrm(ks[6], (DEPTH, POOL_WIDTH)),
        "w_out": nrm(ks[7], (DEPTH, D_MODEL, D_MODEL)) * (D_MODEL ** -0.5) * DEEPNORM_BETA,
        "ln1_g": 1.0 + 0.05 * nrm(ks[8], (DEPTH, D_MODEL)),
        "ln1_b": 0.02 * nrm(ks[9], (DEPTH, D_MODEL)),
        "w_up": nrm(ks[10], (DEPTH, D_MODEL, D_FF)) * D_MODEL ** -0.5,
        "w_down": nrm(ks[11], (DEPTH, D_FF, D_MODEL)) * (D_FF ** -0.5) * DEEPNORM_BETA,
        "ln2_g": 1.0 + 0.05 * nrm(ks[12], (DEPTH, D_MODEL)),
        "ln2_b": 0.02 * nrm(ks[13], (DEPTH, D_MODEL)),
    }


def reference(x, w_in, lb_logits, hgrn_norm_g, w_a, w_pool, pool_scale, w_out,
              ln1_g, ln1_b, w_up, w_down, ln2_g, ln2_b):
    lower_bounds = jnp.cumsum(jax.nn.softmax(lb_logits.astype(jnp.float32), axis=0), axis=0)
    for l in range(DEPTH):
        mix = hybrid_mixer(x, w_in[l], lower_bounds[l], hgrn_norm_g[l], w_a[l], w_pool[l],
                           pool_scale[l], w_out[l])
        x = layer_norm(DEEPNORM_ALPHA * x + mix, ln1_g[l], ln1_b[l])
        x = layer_norm(DEEPNORM_ALPHA * x + sq_relu_mlp(x, w_up[l], w_down[l]), ln2_g[l], ln2_b[l])
    return x
```

```python
import functools

import jax
import jax.numpy as jnp
from jax import lax
from jax.experimental import pallas as pl
from jax.experimental.pallas import tpu as pltpu

D_MODEL = 1024
HEADS = 8
HEAD_DIM = D_MODEL // HEADS
POOL_WINDOWS = (2, 4, 8, 16)
POOL_GROUP_DIM = D_MODEL // len(POOL_WINDOWS)
POOL_HISTORY = 16
D_FF = 4 * D_MODEL
IN_COLS = 7 * D_MODEL
LN_EPS = 1e-5
RMS_EPS = 1e-6

REC_BLOCK = 128
DIAG_CHUNK = 32
MIXER_TOKENS = 256
MLP_TOKENS = 512
VMEM_LIMIT_BYTES = 56 * 1024 * 1024

BF16 = jnp.bfloat16
F32 = jnp.float32


def _sigmoid(z):
    return 0.5 * jnp.tanh(0.5 * z) + 0.5


def _dot(a, b):
    return jnp.dot(a, b, preferred_element_type=F32)


def _dot_nt(a, b):
    return lax.dot_general(a, b, (((1,), (1,)), ((), ())), preferred_element_type=F32)


def _dot_tn(a, b):
    return lax.dot_general(a, b, (((0,), (0,)), ((), ())), preferred_element_type=F32)


def _layer_norm(y, g, b):
    mu = jnp.mean(y, axis=-1, keepdims=True)
    c = y - mu
    var = jnp.mean(c * c, axis=-1, keepdims=True)
    return c * lax.rsqrt(var + LN_EPS) * g + b


def _block_cumsum(tri, z):
    z1 = z.astype(BF16)
    r1 = z - z1.astype(F32)
    z2 = r1.astype(BF16)
    z3 = (r1 - z2.astype(F32)).astype(BF16)
    return _dot(tri, z1) + _dot(tri, z2) + _dot(tri, z3)


def _row_bcast(g, period, row):
    n = g.shape[0] // period
    g3 = g.reshape(n, period, g.shape[1])
    r = jnp.broadcast_to(g3[:, row:row + 1, :], g3.shape)
    return r.reshape(g.shape)


def _mixer_kernel(alpha, x_ref, w_in_ref, lbl_ref, g_ref, wa_ref, wp_ref, ps_ref, wo_ref,
                  l1g_ref, l1b_ref, out_ref, state_ref, carry_ref, o_ref):
    tm = x_ref.shape[1]
    j = pl.program_id(1)

    @pl.when(j == 0)
    def _():
        state_ref[...] = jnp.zeros_like(state_ref)
        carry_ref[...] = jnp.zeros_like(carry_ref)

    x = x_ref[0]
    xb = x.astype(BF16)

    def proj(c):
        return _dot(xb, w_in_ref[:, c * D_MODEL:(c + 1) * D_MODEL])

    l0 = lbl_ref[0:1, :]
    l1 = lbl_ref[1:2, :]
    lmax = jnp.maximum(l0, l1)
    e0 = jnp.exp(l0 - lmax)
    e1 = jnp.exp(l1 - lmax)
    lb = e0 / (e0 + e1)

    q = proj(0)
    q = q * _sigmoid(q) * (HEAD_DIM ** -0.5)
    f = lb + (1.0 - lb) * _sigmoid(proj(1))
    k = 1.0 - f
    logf = jnp.log(f)
    v = proj(2).astype(BF16)

    L = REC_BLOCK
    ti = lax.broadcasted_iota(jnp.int32, (L, L), 0)
    si = lax.broadcasted_iota(jnp.int32, (L, L), 1)
    tri = (si <= ti).astype(BF16)
    m_diag = ((ti // DIAG_CHUNK) == (si // DIAG_CHUNK)) & (si <= ti)
    m_32 = ((ti // 64) == (si // 64)) & ((ti % 64) >= 32) & ((si % 64) < 32)
    m_64 = (ti >= 64) & (si < 64)

    for blk in range(tm // L):
        rows = slice(blk * L, (blk + 1) * L)
        qb, kb, vb = q[rows], k[rows], v[rows]
        G = _block_cumsum(tri, logf[rows])
        g_last = G[L - 1:L, :]
        q_dec = (qb * jnp.exp(G)).astype(BF16)
        k_end = (kb * jnp.exp(g_last - G)).astype(BF16)
        decay = jnp.exp(g_last)
        r_d = _row_bcast(G, DIAG_CHUNK, DIAG_CHUNK // 2 - 1)
        q_d = (qb * jnp.exp(G - r_d)).astype(BF16)
        k_d = (kb * jnp.exp(r_d - G)).astype(BF16)
        e_32 = jnp.exp(-jnp.abs(G - _row_bcast(G, 64, 31)))
        q_32 = (qb * e_32).astype(BF16)
        k_32 = (kb * e_32).astype(BF16)
        e_64 = jnp.exp(-jnp.abs(G - _row_bcast(G, 128, 63)))
        q_64 = (qb * e_64).astype(BF16)
        k_64 = (kb * e_64).astype(BF16)

        for h in range(HEADS):
            cols = slice(h * HEAD_DIM, (h + 1) * HEAD_DIM)
            s_d = _dot_nt(q_d[:, cols], k_d[:, cols])
            s_32 = _dot_nt(q_32[:, cols], k_32[:, cols])
            s_64 = _dot_nt(q_64[:, cols], k_64[:, cols])
            p = jnp.where(m_diag, s_d, jnp.where(m_32, s_32, jnp.where(m_64, s_64, 0.0)))
            state = state_ref[h]
            o_h = _dot(p.astype(BF16), vb[:, cols]) + _dot(q_dec[:, cols], state.astype(BF16))
            decay_col = jnp.broadcast_to(decay[:, cols], (HEAD_DIM, HEAD_DIM)).T
            state_ref[h] = decay_col * state + _dot_tn(k_end[:, cols], vb[:, cols])
            o_h = o_h * lax.rsqrt(jnp.mean(o_h * o_h, axis=-1, keepdims=True) + RMS_EPS)
            o_ref[rows, cols] = o_h

    o = o_ref[...] * g_ref[...] * _sigmoid(proj(3))
    a = _dot(o.astype(BF16), wa_ref[...])

    pv = proj(4)
    ext = jnp.concatenate([carry_ref[...], pv], axis=0)
    carry_ref[...] = pv[tm - POOL_HISTORY:, :]
    pos = j * tm + lax.broadcasted_iota(jnp.int32, (tm, 1), 0)
    b_parts = []
    for gi, w in enumerate(POOL_WINDOWS):
        cols = slice(gi * POOL_GROUP_DIM, (gi + 1) * POOL_GROUP_DIM)
        acc = ext[:, cols]
        span = 1
        while span < w:
            acc = acc + pltpu.roll(acc, span, 0)
            span *= 2
        inv_count = 1.0 / jnp.minimum(pos + 1, w).astype(F32)
        pooled = acc[POOL_HISTORY:, :] * inv_count - pv[:, cols]
        b_parts.append(_dot(pooled.astype(BF16), wp_ref[gi]))
    b = jnp.concatenate(b_parts, axis=-1) * ps_ref[...]

    merged = _sigmoid(proj(5)) * a + _sigmoid(proj(6)) * b
    mix = _dot(merged.astype(BF16), wo_ref[...])
    out_ref[0] = _layer_norm(alpha * x + mix, l1g_ref[...], l1b_ref[...])


def _mlp_kernel(alpha, x_ref, wu_ref, wd_ref, g_ref, b_ref, out_ref):
    x = x_ref[...]
    hdn = jnp.maximum(_dot(x.astype(BF16), wu_ref[...]), 0.0)
    hdn = (hdn * hdn).astype(BF16)
    y = alpha * x + _dot(hdn, wd_ref[...])
    out_ref[...] = _layer_norm(y, g_ref[...], b_ref[...])


def _const_spec(shape):
    nd = len(shape)
    return pl.BlockSpec(shape, lambda *_: (0,) * nd, pipeline_mode=pl.Buffered(1))


def _mixer(x, w_in, lb_logits, norm_g, w_a, w_pool, pool_scale, w_out, ln_g, ln_b, alpha):
    B, S, D = x.shape
    tm = MIXER_TOKENS
    row = lambda p: p.reshape(1, D).astype(F32)
    return pl.pallas_call(
        functools.partial(_mixer_kernel, alpha),
        grid=(B, S // tm),
        in_specs=[
            pl.BlockSpec((1, tm, D), lambda b, j: (b, j, 0)),
            _const_spec((D, IN_COLS)),
            _const_spec(lb_logits.shape),
            _const_spec((1, D)),
            _const_spec((D, D)),
            _const_spec(w_pool.shape),
            _const_spec((1, D)),
            _const_spec((D, D)),
            _const_spec((1, D)),
            _const_spec((1, D)),
        ],
        out_specs=pl.BlockSpec((1, tm, D), lambda b, j: (b, j, 0)),
        out_shape=jax.ShapeDtypeStruct((B, S, D), F32),
        scratch_shapes=[
            pltpu.VMEM((HEADS, HEAD_DIM, HEAD_DIM), F32),
            pltpu.VMEM((POOL_HISTORY, D), F32),
            pltpu.VMEM((tm, D), F32),
        ],
        compiler_params=pltpu.CompilerParams(
            dimension_semantics=("arbitrary", "arbitrary"),
            vmem_limit_bytes=VMEM_LIMIT_BYTES),
        name="hgrn2_pool_mixer",
    )(x, w_in.astype(BF16), lb_logits.astype(F32), row(norm_g), w_a.astype(BF16), w_pool.astype(BF16),
      row(pool_scale), w_out.astype(BF16), row(ln_g), row(ln_b))


def _mlp(x2d, w_up, w_down, ln_g, ln_b, alpha):
    N, D = x2d.shape
    tm = MLP_TOKENS
    row = lambda p: p.reshape(1, D).astype(F32)
    return pl.pallas_call(
        functools.partial(_mlp_kernel, alpha),
        grid=(N // tm,),
        in_specs=[
            pl.BlockSpec((tm, D), lambda i: (i, 0)),
            _const_spec((D, D_FF)),
            _const_spec((D_FF, D)),
            _const_spec((1, D)),
            _const_spec((1, D)),
        ],
        out_specs=pl.BlockSpec((tm, D), lambda i: (i, 0)),
        out_shape=jax.ShapeDtypeStruct((N, D), F32),
        compiler_params=pltpu.CompilerParams(
            dimension_semantics=("arbitrary",),
            vmem_limit_bytes=VMEM_LIMIT_BYTES),
        name="sq_relu_mlp",
    )(x2d, w_up.astype(BF16), w_down.astype(BF16), row(ln_g), row(ln_b))


def kernel(x, w_in, lb_logits, hgrn_norm_g, w_a, w_pool, pool_scale, w_out, ln1_g, ln1_b, w_up, w_down, ln2_g, ln2_b):
    depth = w_in.shape[0]
    assert depth == 1 and lb_logits.shape[0] == depth + 1
    alpha = (2.0 * depth) ** 0.25
    B, S, D = x.shape
    for l in range(depth):
        x = _mixer(x, w_in[l], lb_logits, hgrn_norm_g[l], w_a[l], w_pool[l], pool_scale[l], w_out[l],
                   ln1_g[l], ln1_b[l], alpha)
        x = _mlp(x.reshape(B * S, D), w_up[l], w_down[l], ln2_g[l], ln2_b[l], alpha).reshape(B, S, D)
    return x
```

```python
import functools

import jax
import jax.numpy as jnp
from jax import lax
from jax.experimental import pallas as pl
from jax.experimental.pallas import tpu as pltpu

D_MODEL = 1024
HEADS = 8
HEAD_DIM = D_MODEL // HEADS
POOL_WINDOWS = (2, 4, 8, 16)
POOL_GROUP_DIM = D_MODEL // len(POOL_WINDOWS)
POOL_HISTORY = 16
D_FF = 4 * D_MODEL
IN_COLS = 7 * D_MODEL
LN_EPS = 1e-5
RMS_EPS = 1e-6

REC_BLOCK = 128
DIAG_CHUNK = 32
PRE_FILL = 8
MIXER_TOKENS = 256
MLP_TOKENS = 512
VMEM_LIMIT_BYTES = 56 * 1024 * 1024

BF16 = jnp.bfloat16
F32 = jnp.float32


def _sigmoid(z):
    return 0.5 * jnp.tanh(0.5 * z) + 0.5


def _dot(a, b):
    return jnp.dot(a, b, preferred_element_type=F32)


def _dot_nt(a, b):
    return lax.dot_general(a, b, (((1,), (1,)), ((), ())), preferred_element_type=F32)


def _dot_tn(a, b):
    return lax.dot_general(a, b, (((0,), (0,)), ((), ())), preferred_element_type=F32)


def _layer_norm(y, g, b):
    mu = jnp.mean(y, axis=-1, keepdims=True)
    c = y - mu
    var = jnp.mean(c * c, axis=-1, keepdims=True)
    return c * lax.rsqrt(var + LN_EPS) * g + b


def _block_cumsum(tri, z):
    z1 = z.astype(BF16)
    r1 = z - z1.astype(F32)
    z2 = r1.astype(BF16)
    z3 = (r1 - z2.astype(F32)).astype(BF16)
    return _dot(tri, z1) + _dot(tri, z2) + _dot(tri, z3)


def _row_bcast(g, period, row):
    n = g.shape[0] // period
    g3 = g.reshape(n, period, g.shape[1])
    r = jnp.broadcast_to(g3[:, row:row + 1, :], g3.shape)
    return r.reshape(g.shape)


def _mixer_kernel(alpha, x_ref, w_in_ref, lbl_ref, g_ref, wa_ref, wp_ref, ps_ref, wo_ref,
                  l1g_ref, l1b_ref, out_ref, state_ref, carry_ref, o_ref):
    tm = x_ref.shape[1]
    j = pl.program_id(1)

    @pl.when(j == 0)
    def _():
        state_ref[...] = jnp.zeros_like(state_ref)
        carry_ref[...] = jnp.zeros_like(carry_ref)

    x = x_ref[0]
    xb = x.astype(BF16)

    def proj(c):
        return _dot(xb, w_in_ref[:, c * D_MODEL:(c + 1) * D_MODEL])

    l0 = lbl_ref[0:1, :]
    l1 = lbl_ref[1:2, :]
    lmax = jnp.maximum(l0, l1)
    e0 = jnp.exp(l0 - lmax)
    e1 = jnp.exp(l1 - lmax)
    lb = e0 / (e0 + e1)

    f = lb + (1.0 - lb) * _sigmoid(proj(1))
    k = 1.0 - f
    logf = jnp.log(f)
    q = proj(0)
    q = q * _sigmoid(q) * (HEAD_DIM ** -0.5)
    v = proj(2).astype(BF16)

    L = REC_BLOCK
    PW = 2 * HEAD_DIM
    ti = lax.broadcasted_iota(jnp.int32, (L, L), 0)
    si = lax.broadcasted_iota(jnp.int32, (L, L), 1)
    tri = (si <= ti).astype(BF16)
    tp = lax.broadcasted_iota(jnp.int32, (L, PW), 0)
    lp = lax.broadcasted_iota(jnp.int32, (L, PW), 1)
    sp = lp % L
    m_diag = ((tp // DIAG_CHUNK) == (sp // DIAG_CHUNK)) & (sp <= tp)
    m_32 = ((tp // 64) == (sp // 64)) & ((tp % 64) >= 32) & ((sp % 64) < 32)
    m_64 = (tp >= 64) & (sp < 64)
    head_a = lp < HEAD_DIM
    rr = lax.broadcasted_iota(jnp.int32, (PW, PW), 0)
    cc = lax.broadcasted_iota(jnp.int32, (PW, PW), 1)
    same_head = (rr < HEAD_DIM) == (cc < HEAD_DIM)

    def block_diag(pair):
        z = jnp.zeros_like(pair)
        return jnp.concatenate([jnp.where(head_a, pair, z), jnp.where(head_a, z, pair)], axis=0)

    n_blk = tm // L
    row_slices = [slice(i * L, (i + 1) * L) for i in range(n_blk)]
    cums = [_block_cumsum(tri, logf[rows]) for rows in row_slices]

    def prepare(blk):
        rows = row_slices[blk]
        qb, kb, G = q[rows], k[rows], cums[blk]
        g_last = G[L - 1:L, :]
        ops = dict(
            v=v[rows],
            q_dec=(qb * jnp.exp(G)).astype(BF16),
            k_end=(kb * jnp.exp(g_last - G)).astype(BF16),
            decay=jnp.exp(g_last))
        r_d = _row_bcast(G, DIAG_CHUNK, DIAG_CHUNK // 2 - 1)
        ops["q_d"] = (qb * jnp.exp(G - r_d)).astype(BF16)
        ops["k_d"] = (kb * jnp.exp(r_d - G)).astype(BF16)
        e_32 = jnp.exp(-jnp.abs(G - _row_bcast(G, 64, 31)))
        ops["q_32"] = (qb * e_32).astype(BF16)
        ops["k_32"] = (kb * e_32).astype(BF16)
        e_64 = jnp.exp(-jnp.abs(G - _row_bcast(G, 128, 63)))
        ops["q_64"] = (qb * e_64).astype(BF16)
        ops["k_64"] = (kb * e_64).astype(BF16)
        return ops

    filler_cols = [(c, n) for c in (3, 4, 5, 6) for n in range(D_MODEL // PW)]
    filler_out = {}

    def fill():
        if filler_cols:
            c, n = filler_cols.pop(0)
            lo = c * D_MODEL + n * PW
            filler_out[(c, n)] = _dot(xb, w_in_ref[:, lo:lo + PW])

    def recur(blk, ops):
        rows = row_slices[blk]
        probs = []
        for p in range(HEADS // 2):
            cols = slice(p * PW, (p + 1) * PW)
            s_d = _dot_nt(ops["q_d"][:, cols], block_diag(ops["k_d"][:, cols]))
            s_32 = _dot_nt(ops["q_32"][:, cols], block_diag(ops["k_32"][:, cols]))
            s_64 = _dot_nt(ops["q_64"][:, cols], block_diag(ops["k_64"][:, cols]))
            if p % 2 == 1:
                fill()
            probs.append(jnp.where(m_diag, s_d, jnp.where(m_32, s_32, jnp.where(m_64, s_64, 0.0))).astype(BF16))
        for p in range(HEADS // 2):
            cols = slice(p * PW, (p + 1) * PW)
            state = state_ref[p]
            v_pair = ops["v"][:, cols]
            o_p = _dot(probs[p], block_diag(v_pair)) + _dot(ops["q_dec"][:, cols], state.astype(BF16))
            decay_col = jnp.broadcast_to(ops["decay"][:, cols], (PW, PW)).T
            kv = _dot_tn(ops["k_end"][:, cols], v_pair)
            if p % 2 == 1:
                fill()
            state_ref[p] = decay_col * state + jnp.where(same_head, kv, 0.0)
            for half in range(2):
                o_h = o_p[:, half * HEAD_DIM:(half + 1) * HEAD_DIM]
                o_h = o_h * lax.rsqrt(jnp.mean(o_h * o_h, axis=-1, keepdims=True) + RMS_EPS)
                o_ref[rows, p * PW + half * HEAD_DIM:p * PW + (half + 1) * HEAD_DIM] = o_h

    for blk in range(n_blk):
        ops = prepare(blk)
        if blk == 0:
            for _ in range(PRE_FILL):
                fill()
        recur(blk, ops)
    while filler_cols:
        fill()
    full = lambda c: jnp.concatenate([filler_out[(c, n)] for n in range(D_MODEL // PW)], axis=-1)
    sig_og, pv, sig_ga, sig_gb = _sigmoid(full(3)), full(4), _sigmoid(full(5)), _sigmoid(full(6))

    ext = jnp.concatenate([carry_ref[...], pv], axis=0)
    carry_ref[...] = pv[tm - POOL_HISTORY:, :]
    pos = j * tm + lax.broadcasted_iota(jnp.int32, (tm, 1), 0)
    b_parts = []
    for gi, w in enumerate(POOL_WINDOWS):
        cols = slice(gi * POOL_GROUP_DIM, (gi + 1) * POOL_GROUP_DIM)
        acc = ext[:, cols]
        span = 1
        while span < w:
            acc = acc + pltpu.roll(acc, span, 0)
            span *= 2
        inv_count = 1.0 / jnp.minimum(pos + 1, w).astype(F32)
        pooled = acc[POOL_HISTORY:, :] * inv_count - pv[:, cols]
        b_parts.append(_dot(pooled.astype(BF16), wp_ref[gi]))
    b = jnp.concatenate(b_parts, axis=-1) * ps_ref[...]

    o = o_ref[...] * g_ref[...] * sig_og
    a = _dot(o.astype(BF16), wa_ref[...])
    merged = sig_ga * a + sig_gb * b
    mix = _dot(merged.astype(BF16), wo_ref[...])
    out_ref[0] = _layer_norm(alpha * x + mix, l1g_ref[...], l1b_ref[...])


def _mlp_kernel(alpha, x_ref, wu_ref, wd_ref, g_ref, b_ref, out_ref):
    x = x_ref[...]
    hdn = jnp.maximum(_dot(x.astype(BF16), wu_ref[...]), 0.0)
    hdn = (hdn * hdn).astype(BF16)
    y = alpha * x + _dot(hdn, wd_ref[...])
    out_ref[...] = _layer_norm(y, g_ref[...], b_ref[...])


def _const_spec(shape):
    nd = len(shape)
    return pl.BlockSpec(shape, lambda *_: (0,) * nd, pipeline_mode=pl.Buffered(1))


def _mixer(x, w_in, lb_logits, norm_g, w_a, w_pool, pool_scale, w_out, ln_g, ln_b, alpha):
    B, S, D = x.shape
    tm = MIXER_TOKENS
    row = lambda p: p.reshape(1, D).astype(F32)
    return pl.pallas_call(
        functools.partial(_mixer_kernel, alpha),
        grid=(B, S // tm),
        in_specs=[
            pl.BlockSpec((1, tm, D), lambda b, j: (b, j, 0)),
            _const_spec((D, IN_COLS)),
            _const_spec(lb_logits.shape),
            _const_spec((1, D)),
            _const_spec((D, D)),
            _const_spec(w_pool.shape),
            _const_spec((1, D)),
            _const_spec((D, D)),
            _const_spec((1, D)),
            _const_spec((1, D)),
        ],
        out_specs=pl.BlockSpec((1, tm, D), lambda b, j: (b, j, 0)),
        out_shape=jax.ShapeDtypeStruct((B, S, D), F32),
        scratch_shapes=[
            pltpu.VMEM((HEADS // 2, 2 * HEAD_DIM, 2 * HEAD_DIM), F32),
            pltpu.VMEM((POOL_HISTORY, D), F32),
            pltpu.VMEM((tm, D), F32),
        ],
        compiler_params=pltpu.CompilerParams(
            dimension_semantics=("arbitrary", "arbitrary"),
            vmem_limit_bytes=VMEM_LIMIT_BYTES),
        name="hgrn2_pool_mixer",
    )(x, w_in.astype(BF16), lb_logits.astype(F32), row(norm_g), w_a.astype(BF16), w_pool.astype(BF16),
      row(pool_scale), w_out.astype(BF16), row(ln_g), row(ln_b))


def _mlp(x2d, w_up, w_down, ln_g, ln_b, alpha):
    N, D = x2d.shape
    tm = MLP_TOKENS
    row = lambda p: p.reshape(1, D).astype(F32)
    return pl.pallas_call(
        functools.partial(_mlp_kernel, alpha),
        grid=(N // tm,),
        in_specs=[
            pl.BlockSpec((tm, D), lambda i: (i, 0)),
            _const_spec((D, D_FF)),
            _const_spec((D_FF, D)),
            _const_spec((1, D)),
            _const_spec((1, D)),
        ],
        out_specs=pl.BlockSpec((tm, D), lambda i: (i, 0)),
        out_shape=jax.ShapeDtypeStruct((N, D), F32),
        compiler_params=pltpu.CompilerParams(
            dimension_semantics=("arbitrary",),
            vmem_limit_bytes=VMEM_LIMIT_BYTES),
        name="sq_relu_mlp",
    )(x2d, w_up.astype(BF16), w_down.astype(BF16), row(ln_g), row(ln_b))


def kernel(x, w_in, lb_logits, hgrn_norm_g, w_a, w_pool, pool_scale, w_out, ln1_g, ln1_b, w_up, w_down, ln2_g, ln2_b):
    depth = w_in.shape[0]
    assert depth == 1 and lb_logits.shape[0] == depth + 1
    alpha = (2.0 * depth) ** 0.25
    B, S, D = x.shape
    for l in range(depth):
        x = _mixer(x, w_in[l], lb_logits, hgrn_norm_g[l], w_a[l], w_pool[l], pool_scale[l], w_out[l],
                   ln1_g[l], ln1_b[l], alpha)
        x = _mlp(x.reshape(B * S, D), w_up[l], w_down[l], ln2_g[l], ln2_b[l], alpha).reshape(B, S, D)
    return x
```

```python
import functools

import jax
import jax.numpy as jnp
from jax import lax
from jax.experimental import pallas as pl
from jax.experimental.pallas import tpu as pltpu

D_MODEL = 1024
HEADS = 8
HEAD_DIM = D_MODEL // HEADS
POOL_WINDOWS = (2, 4, 8, 16)
POOL_GROUP_DIM = D_MODEL // len(POOL_WINDOWS)
POOL_HISTORY = 16
D_FF = 4 * D_MODEL
IN_COLS = 7 * D_MODEL
LN_EPS = 1e-5
RMS_EPS = 1e-6

REC_BLOCK = 128
DIAG_CHUNK = 32
TILE_TOKENS = 256
FILL_COLS = 512
FF_CHUNK = 1024
VMEM_LIMIT_BYTES = 60 * 1024 * 1024

BF16 = jnp.bfloat16
F32 = jnp.float32


def _sigmoid(z):
    return 0.5 * jnp.tanh(0.5 * z) + 0.5


def _dot(a, b):
    return jnp.dot(a, b, preferred_element_type=F32)


def _dot_nt(a, b):
    return lax.dot_general(a, b, (((1,), (1,)), ((), ())), preferred_element_type=F32)


def _dot_tn(a, b):
    return lax.dot_general(a, b, (((0,), (0,)), ((), ())), preferred_element_type=F32)


def _layer_norm(y, g, b):
    mu = jnp.mean(y, axis=-1, keepdims=True)
    c = y - mu
    var = jnp.mean(c * c, axis=-1, keepdims=True)
    return c * lax.rsqrt(var + LN_EPS) * g + b


def _block_cumsum(tri, z):
    z1 = z.astype(BF16)
    r1 = z - z1.astype(F32)
    z2 = r1.astype(BF16)
    return _dot(tri, z1) + _dot(tri, z2)


def _row_bcast(g, period, row):
    n = g.shape[0] // period
    g3 = g.reshape(n, period, g.shape[1])
    r = jnp.broadcast_to(g3[:, row:row + 1, :], g3.shape)
    return r.reshape(g.shape)


def _block_kernel(alpha, seq_tiles, x_ref, w_in_ref, lbl_ref, g_ref, wa_ref, wp_ref, ps_ref, wo_ref,
                  l1g_ref, l1b_ref, wu_ref, wd_ref, l2g_ref, l2b_ref, out_ref,
                  state_ref, carry_ref, o_ref, y1_ref):
    tm = x_ref.shape[0]
    t = pl.program_id(0)
    j = t % seq_tiles

    @pl.when(t == 0)
    def _():
        y1_ref[...] = jnp.zeros_like(y1_ref)

    @pl.when(j == 0)
    def _():
        state_ref[...] = jnp.zeros_like(state_ref)
        carry_ref[...] = jnp.zeros_like(carry_ref)

    x = x_ref[...]
    xb = x.astype(BF16)

    def proj(c):
        return _dot(xb, w_in_ref[:, c * D_MODEL:(c + 1) * D_MODEL])

    l0 = lbl_ref[0:1, :]
    l1 = lbl_ref[1:2, :]
    lmax = jnp.maximum(l0, l1)
    e0 = jnp.exp(l0 - lmax)
    e1 = jnp.exp(l1 - lmax)
    lb = e0 / (e0 + e1)

    f = lb + (1.0 - lb) * _sigmoid(proj(1))
    k = 1.0 - f
    logf = jnp.log(f)
    q = proj(0)
    q = q * _sigmoid(q) * (HEAD_DIM ** -0.5)
    v = proj(2).astype(BF16)

    L = REC_BLOCK
    PW = 2 * HEAD_DIM
    ti = lax.broadcasted_iota(jnp.int32, (L, L), 0)
    si = lax.broadcasted_iota(jnp.int32, (L, L), 1)
    tri = (si <= ti).astype(BF16)
    tp = lax.broadcasted_iota(jnp.int32, (L, PW), 0)
    lp = lax.broadcasted_iota(jnp.int32, (L, PW), 1)
    sp = lp % L
    m_diag = ((tp // DIAG_CHUNK) == (sp // DIAG_CHUNK)) & (sp <= tp)
    m_32 = ((tp // 64) == (sp // 64)) & ((tp % 64) >= 32) & ((sp % 64) < 32)
    m_64 = (tp >= 64) & (sp < 64)
    head_a = lp < HEAD_DIM
    rr = lax.broadcasted_iota(jnp.int32, (PW, PW), 0)
    cc = lax.broadcasted_iota(jnp.int32, (PW, PW), 1)
    same_head = (rr < HEAD_DIM) == (cc < HEAD_DIM)

    def block_diag(pair):
        z = jnp.zeros_like(pair)
        return jnp.concatenate([jnp.where(head_a, pair, z), jnp.where(head_a, z, pair)], axis=0)

    n_blk = tm // L
    row_slices = [slice(i * L, (i + 1) * L) for i in range(n_blk)]
    cums = [_block_cumsum(tri, logf[rows]) for rows in row_slices]

    def prepare(blk):
        rows = row_slices[blk]
        qb, kb, G = q[rows], k[rows], cums[blk]
        g_last = G[L - 1:L, :]
        ops = dict(
            v=v[rows],
            q_dec=(qb * jnp.exp(G)).astype(BF16),
            k_end=(kb * jnp.exp(g_last - G)).astype(BF16),
            decay=jnp.exp(g_last))
        r_d = _row_bcast(G, DIAG_CHUNK, DIAG_CHUNK // 2 - 1)
        ops["q_d"] = (qb * jnp.exp(G - r_d)).astype(BF16)
        ops["k_d"] = (kb * jnp.exp(r_d - G)).astype(BF16)
        e_32 = jnp.exp(-jnp.abs(G - _row_bcast(G, 64, 31)))
        ops["q_32"] = (qb * e_32).astype(BF16)
        ops["k_32"] = (kb * e_32).astype(BF16)
        e_64 = jnp.exp(-jnp.abs(G - _row_bcast(G, 128, 63)))
        ops["q_64"] = (qb * e_64).astype(BF16)
        ops["k_64"] = (kb * e_64).astype(BF16)
        return ops

    units = []
    proj_out = {}

    def proj_unit(c, n):
        def run():
            lo = c * D_MODEL + n * FILL_COLS
            proj_out[(c, n)] = _dot(xb, w_in_ref[:, lo:lo + FILL_COLS])
        return run

    x1 = _layer_norm(y1_ref[...], l1g_ref[...], l1b_ref[...])
    x1b = x1.astype(BF16)
    n_ff = D_FF // FF_CHUNK
    h_parts, d_parts = {}, {}

    def up_unit(c, n):
        def run():
            lo = c * FF_CHUNK + n * FILL_COLS
            hh = jnp.maximum(_dot(x1b, wu_ref[:, lo:lo + FILL_COLS]), 0.0)
            h_parts[(c, n)] = (hh * hh).astype(BF16)
        return run

    def down_unit(c, m):
        def run():
            hc = jnp.concatenate([h_parts[(c, n)] for n in range(FF_CHUNK // FILL_COLS)], axis=-1)
            d_parts[(c, m)] = _dot(hc, wd_ref[c * FF_CHUNK:(c + 1) * FF_CHUNK, m * FILL_COLS:(m + 1) * FILL_COLS])
        return run

    ups = [[up_unit(c, n) for n in range(FF_CHUNK // FILL_COLS)] for c in range(n_ff)]
    downs = [[down_unit(c, m) for m in range(D_MODEL // FILL_COLS)] for c in range(n_ff)]
    projs = [[proj_unit(c, n) for n in range(D_MODEL // FILL_COLS)] for c in (3, 4, 5, 6)]
    units += ups[0] + projs[0] + ups[1] + downs[0] + projs[1] + ups[2] + downs[1] + projs[2]
    units += ups[3] + downs[2] + projs[3] + downs[3]

    def fill(n=1):
        for _ in range(n):
            if units:
                units.pop(0)()

    def recur(blk, ops):
        rows = row_slices[blk]
        probs = []
        for p in range(HEADS // 2):
            cols = slice(p * PW, (p + 1) * PW)
            s_d = _dot_nt(ops["q_d"][:, cols], block_diag(ops["k_d"][:, cols]))
            s_32 = _dot_nt(ops["q_32"][:, cols], block_diag(ops["k_32"][:, cols]))
            s_64 = _dot_nt(ops["q_64"][:, cols], block_diag(ops["k_64"][:, cols]))
            fill()
            probs.append(jnp.where(m_diag, s_d, jnp.where(m_32, s_32, jnp.where(m_64, s_64, 0.0))).astype(BF16))
        for p in range(HEADS // 2):
            cols = slice(p * PW, (p + 1) * PW)
            state = state_ref[p]
            v_pair = ops["v"][:, cols]
            o_p = _dot(probs[p], block_diag(v_pair)) + _dot(ops["q_dec"][:, cols], state.astype(BF16))
            decay_col = jnp.broadcast_to(ops["decay"][:, cols], (PW, PW)).T
            kv = _dot_tn(ops["k_end"][:, cols], v_pair)
            fill()
            state_ref[p] = decay_col * state + jnp.where(same_head, kv, 0.0)
            for half in range(2):
                o_h = o_p[:, half * HEAD_DIM:(half + 1) * HEAD_DIM]
                o_h = o_h * lax.rsqrt(jnp.mean(o_h * o_h, axis=-1, keepdims=True) + RMS_EPS)
                o_ref[rows, p * PW + half * HEAD_DIM:p * PW + (half + 1) * HEAD_DIM] = o_h

    for blk in range(n_blk):
        ops = prepare(blk)
        fill(4)
        recur(blk, ops)
    fill(len(units))
    full = lambda c: jnp.concatenate([proj_out[(c, n)] for n in range(D_MODEL // FILL_COLS)], axis=-1)
    sig_og, pv, sig_ga, sig_gb = _sigmoid(full(3)), full(4), _sigmoid(full(5)), _sigmoid(full(6))

    ext = jnp.concatenate([carry_ref[...], pv], axis=0)
    carry_ref[...] = pv[tm - POOL_HISTORY:, :]
    pos = j * tm + lax.broadcasted_iota(jnp.int32, (tm, 1), 0)
    b_parts = []
    for gi, w in enumerate(POOL_WINDOWS):
        cols = slice(gi * POOL_GROUP_DIM, (gi + 1) * POOL_GROUP_DIM)
        acc = ext[:, cols]
        span = 1
        while span < w:
            acc = acc + pltpu.roll(acc, span, 0)
            span *= 2
        inv_count = 1.0 / jnp.minimum(pos + 1, w).astype(F32)
        pooled = acc[POOL_HISTORY:, :] * inv_count - pv[:, cols]
        b_parts.append(_dot(pooled.astype(BF16), wp_ref[gi]))
    b = jnp.concatenate(b_parts, axis=-1) * ps_ref[...]

    mlp = jnp.concatenate([sum(d_parts[(c, m)] for c in range(n_ff)) for m in range(D_MODEL // FILL_COLS)], axis=-1)
    out_ref[...] = _layer_norm(alpha * x1 + mlp, l2g_ref[...], l2b_ref[...])

    o = o_ref[...] * g_ref[...] * sig_og
    a = _dot(o.astype(BF16), wa_ref[...])
    merged = sig_ga * a + sig_gb * b
    mix = _dot(merged.astype(BF16), wo_ref[...])
    y1_ref[...] = alpha * x + mix


def _const_spec(shape):
    nd = len(shape)
    return pl.BlockSpec(shape, lambda *_: (0,) * nd, pipeline_mode=pl.Buffered(1))


def _layer(x2d, seq_len, w_in, lb_logits, norm_g, w_a, w_pool, pool_scale, w_out, ln1_g, ln1_b,
           w_up, w_down, ln2_g, ln2_b, alpha):
    N, D = x2d.shape
    tm = TILE_TOKENS
    n_tiles = N // tm
    row = lambda p: p.reshape(1, D).astype(F32)
    return pl.pallas_call(
        functools.partial(_block_kernel, alpha, seq_len // tm),
        grid=(n_tiles + 1,),
        in_specs=[
            pl.BlockSpec((tm, D), lambda t: (jnp.minimum(t, n_tiles - 1), 0)),
            _const_spec((D, IN_COLS)),
            _const_spec(lb_logits.shape),
            _const_spec((1, D)),
            _const_spec((D, D)),
            _const_spec(w_pool.shape),
            _const_spec((1, D)),
            _const_spec((D, D)),
            _const_spec((1, D)),
            _const_spec((1, D)),
            _const_spec((D, D_FF)),
            _const_spec((D_FF, D)),
            _const_spec((1, D)),
            _const_spec((1, D)),
        ],
        out_specs=pl.BlockSpec((tm, D), lambda t: (jnp.maximum(t - 1, 0), 0)),
        out_shape=jax.ShapeDtypeStruct((N, D), F32),
        scratch_shapes=[
            pltpu.VMEM((HEADS // 2, 2 * HEAD_DIM, 2 * HEAD_DIM), F32),
            pltpu.VMEM((POOL_HISTORY, D), F32),
            pltpu.VMEM((tm, D), F32),
            pltpu.VMEM((tm, D), F32),
        ],
        compiler_params=pltpu.CompilerParams(
            dimension_semantics=("arbitrary",),
            vmem_limit_bytes=VMEM_LIMIT_BYTES),
        name="hgrn2_pool_mlp_block",
    )(x2d, w_in.astype(BF16), lb_logits.astype(F32), row(norm_g), w_a.astype(BF16), w_pool.astype(BF16),
      row(pool_scale), w_out.astype(BF16), row(ln1_g), row(ln1_b), w_up.astype(BF16), w_down.astype(BF16),
      row(ln2_g), row(ln2_b))


def kernel(x, w_in, lb_logits, hgrn_norm_g, w_a, w_pool, pool_scale, w_out, ln1_g, ln1_b, w_up, w_down, ln2_g, ln2_b):
    depth = w_in.shape[0]
    assert depth == 1 and lb_logits.shape[0] == depth + 1
    alpha = (2.0 * depth) ** 0.25
    B, S, D = x.shape
    x2d = x.reshape(B * S, D)
    for l in range(depth):
        x2d = _layer(x2d, S, w_in[l], lb_logits, hgrn_norm_g[l], w_a[l], w_pool[l], pool_scale[l], w_out[l],
                     ln1_g[l], ln1_b[l], w_up[l], w_down[l], ln2_g[l], ln2_b[l], alpha)
    return x2d.reshape(B, S, D)
```

```python
import functools

import jax
import jax.numpy as jnp
from jax import lax
from jax.experimental import pallas as pl
from jax.experimental.pallas import tpu as pltpu

D_MODEL = 1024
HEADS = 8
HEAD_DIM = D_MODEL // HEADS
POOL_WINDOWS = (2, 4, 8, 16)
POOL_GROUP_DIM = D_MODEL // len(POOL_WINDOWS)
POOL_HISTORY = 16
D_FF = 4 * D_MODEL
IN_COLS = 7 * D_MODEL
LN_EPS = 1e-5
RMS_EPS = 1e-6

REC_BLOCK = 128
DIAG_CHUNK = 32
TILE_TOKENS = 256
FILL_COLS = 512
FF_CHUNK = 1024
VMEM_LIMIT_BYTES = 60 * 1024 * 1024

BF16 = jnp.bfloat16
F32 = jnp.float32


def _sigmoid(z):
    return 0.5 * jnp.tanh(0.5 * z) + 0.5


def _dot(a, b):
    return jnp.dot(a, b, preferred_element_type=F32)


def _dot_nt(a, b):
    return lax.dot_general(a, b, (((1,), (1,)), ((), ())), preferred_element_type=F32)


def _dot_tn(a, b):
    return lax.dot_general(a, b, (((0,), (0,)), ((), ())), preferred_element_type=F32)


def _layer_norm(y, g, b):
    mu = jnp.mean(y, axis=-1, keepdims=True)
    c = y - mu
    var = jnp.mean(c * c, axis=-1, keepdims=True)
    return c * lax.rsqrt(var + LN_EPS) * g + b


def _block_cumsum(tri2, z):
    z1 = z.astype(BF16)
    z2 = (z - z1.astype(F32)).astype(BF16)
    return _dot(tri2, jnp.concatenate([z1, z2], axis=0))


def _row_bcast(g, period, row):
    n = g.shape[0] // period
    g3 = g.reshape(n, period, g.shape[1])
    r = jnp.broadcast_to(g3[:, row:row + 1, :], g3.shape)
    return r.reshape(g.shape)


def _block_kernel(alpha, seq_tiles, x_ref, w_in_ref, lbl_ref, g_ref, wa_ref, wp_ref, ps_ref, wo_ref,
                  l1g_ref, l1b_ref, wu_ref, wd_ref, l2g_ref, l2b_ref, out_ref,
                  state_ref, carry_ref, o_ref, y1_ref):
    tm = x_ref.shape[0]
    t = pl.program_id(0)
    j = t % seq_tiles

    @pl.when(t == 0)
    def _():
        y1_ref[...] = jnp.zeros_like(y1_ref)

    @pl.when(j == 0)
    def _():
        state_ref[...] = jnp.zeros_like(state_ref)
        carry_ref[...] = jnp.zeros_like(carry_ref)

    x = x_ref[...]
    xb = x.astype(BF16)

    def proj(c):
        return _dot(xb, w_in_ref[:, c * D_MODEL:(c + 1) * D_MODEL])

    l0 = lbl_ref[0:1, :]
    l1 = lbl_ref[1:2, :]
    lmax = jnp.maximum(l0, l1)
    e0 = jnp.exp(l0 - lmax)
    e1 = jnp.exp(l1 - lmax)
    lb = e0 / (e0 + e1)
    f_mid = 0.5 * (1.0 + lb)
    f_half = 0.5 * (1.0 - lb)

    f_swing = f_half * jnp.tanh(0.5 * proj(1))
    k = f_half - f_swing
    logf = jnp.log(f_mid + f_swing)
    q = proj(0)
    q = (q * (0.5 * HEAD_DIM ** -0.5)) * (jnp.tanh(0.5 * q) + 1.0)
    v = proj(2).astype(BF16)

    L = REC_BLOCK
    assert tm == 2 * L
    PW = 2 * HEAD_DIM
    ti = lax.broadcasted_iota(jnp.int32, (L, 2 * L), 0)
    si = lax.broadcasted_iota(jnp.int32, (L, 2 * L), 1)
    tri2 = ((si % L) <= ti).astype(BF16)
    tp = lax.broadcasted_iota(jnp.int32, (L, PW), 0)
    lp = lax.broadcasted_iota(jnp.int32, (L, PW), 1)
    sp = lp % L
    m_diag = ((tp // DIAG_CHUNK) == (sp // DIAG_CHUNK)) & (sp <= tp)
    m_32 = ((tp // 64) == (sp // 64)) & ((tp % 64) >= 32) & ((sp % 64) < 32)
    m_64 = (tp >= 64) & (sp < 64)
    head_a = lp < HEAD_DIM
    rr = lax.broadcasted_iota(jnp.int32, (PW, PW), 0)
    cc = lax.broadcasted_iota(jnp.int32, (PW, PW), 1)
    same_head = (rr < HEAD_DIM) == (cc < HEAD_DIM)

    def block_diag(pair):
        z = jnp.zeros_like(pair)
        return jnp.concatenate([jnp.where(head_a, pair, z), jnp.where(head_a, z, pair)], axis=0)

    n_blk = tm // L
    row_slices = [slice(i * L, (i + 1) * L) for i in range(n_blk)]
    cums = [_block_cumsum(tri2, logf[rows]) for rows in row_slices]

    def prepare(blk):
        rows = row_slices[blk]
        qb, kb, G = q[rows], k[rows], cums[blk]
        g_last = G[L - 1:L, :]
        ops = dict(
            v=v[rows],
            q_dec=(qb * jnp.exp(G)).astype(BF16),
            k_end=(kb * jnp.exp(g_last - G)).astype(BF16),
            decay=jnp.exp(g_last))
        r_d = _row_bcast(G, DIAG_CHUNK, DIAG_CHUNK // 2 - 1)
        ops["q_d"] = (qb * jnp.exp(G - r_d)).astype(BF16)
        ops["k_d"] = (kb * jnp.exp(r_d - G)).astype(BF16)
        e_32 = jnp.exp(-jnp.abs(G - _row_bcast(G, 64, 31)))
        ops["q_32"] = (qb * e_32).astype(BF16)
        ops["k_32"] = (kb * e_32).astype(BF16)
        e_64 = jnp.exp(-jnp.abs(G - _row_bcast(G, 128, 63)))
        ops["q_64"] = (qb * e_64).astype(BF16)
        ops["k_64"] = (kb * e_64).astype(BF16)
        return ops

    units = []
    proj_out = {}

    def proj_unit(c, n):
        def run():
            lo = c * D_MODEL + n * FILL_COLS
            proj_out[(c, n)] = _dot(xb, w_in_ref[:, lo:lo + FILL_COLS])
        return run

    x1 = _layer_norm(y1_ref[...], l1g_ref[...], l1b_ref[...])
    x1b = x1.astype(BF16)
    n_ff = D_FF // FF_CHUNK
    h_parts, d_parts = {}, {}

    def up_unit(c, n):
        def run():
            lo = c * FF_CHUNK + n * FILL_COLS
            hh = jnp.maximum(_dot(x1b, wu_ref[:, lo:lo + FILL_COLS]), 0.0)
            h_parts[(c, n)] = (hh * hh).astype(BF16)
        return run

    def down_unit(c, m):
        def run():
            hc = jnp.concatenate([h_parts[(c, n)] for n in range(FF_CHUNK // FILL_COLS)], axis=-1)
            d_parts[(c, m)] = _dot(hc, wd_ref[c * FF_CHUNK:(c + 1) * FF_CHUNK, m * FILL_COLS:(m + 1) * FILL_COLS])
        return run

    ups = [[up_unit(c, n) for n in range(FF_CHUNK // FILL_COLS)] for c in range(n_ff)]
    downs = [[down_unit(c, m) for m in range(D_MODEL // FILL_COLS)] for c in range(n_ff)]
    projs = [[proj_unit(c, n) for n in range(D_MODEL // FILL_COLS)] for c in (3, 4, 5, 6)]
    units += projs[0] + projs[1] + projs[2] + projs[3]
    units += ups[0] + ups[1] + downs[0] + ups[2] + downs[1] + ups[3] + downs[2] + downs[3]

    def fill(n=1):
        for _ in range(n):
            if units:
                units.pop(0)()

    def recur(blk, ops):
        rows = row_slices[blk]
        probs = []
        for p in range(HEADS // 2):
            cols = slice(p * PW, (p + 1) * PW)
            s_d = _dot_nt(ops["q_d"][:, cols], block_diag(ops["k_d"][:, cols]))
            s_32 = _dot_nt(ops["q_32"][:, cols], block_diag(ops["k_32"][:, cols]))
            s_64 = _dot_nt(ops["q_64"][:, cols], block_diag(ops["k_64"][:, cols]))
            fill()
            probs.append(jnp.where(m_diag, s_d, jnp.where(m_32, s_32, jnp.where(m_64, s_64, 0.0))).astype(BF16))
        for p in range(HEADS // 2):
            cols = slice(p * PW, (p + 1) * PW)
            state = state_ref[p]
            v_pair = ops["v"][:, cols]
            o_p = _dot(probs[p], block_diag(v_pair)) + _dot(ops["q_dec"][:, cols], state.astype(BF16))
            decay_col = jnp.broadcast_to(ops["decay"][:, cols], (PW, PW)).T
            kv = _dot_tn(ops["k_end"][:, cols], v_pair)
            fill()
            state_ref[p] = decay_col * state + jnp.where(same_head, kv, 0.0)
            for half in range(2):
                o_h = o_p[:, half * HEAD_DIM:(half + 1) * HEAD_DIM]
                o_h = o_h * lax.rsqrt(jnp.mean(o_h * o_h, axis=-1, keepdims=True) + RMS_EPS)
                o_ref[rows, p * PW + half * HEAD_DIM:p * PW + (half + 1) * HEAD_DIM] = o_h

    def pool_branch():
        pv = jnp.concatenate([proj_out[(4, n)] for n in range(D_MODEL // FILL_COLS)], axis=-1)
        ext = jnp.concatenate([carry_ref[...], pv], axis=0)
        carry_ref[...] = pv[tm - POOL_HISTORY:, :]
        pos = j * tm + lax.broadcasted_iota(jnp.int32, (tm, 1), 0)
        b_parts = []
        for gi, w in enumerate(POOL_WINDOWS):
            cols = slice(gi * POOL_GROUP_DIM, (gi + 1) * POOL_GROUP_DIM)
            acc = ext[:, cols]
            span = 1
            while span < w:
                acc = acc + pltpu.roll(acc, span, 0)
                span *= 2
            inv_count = 1.0 / jnp.minimum(pos + 1, w).astype(F32)
            pooled = acc[POOL_HISTORY:, :] * inv_count - pv[:, cols]
            b_parts.append(_dot(pooled.astype(BF16), wp_ref[gi]))
        return jnp.concatenate(b_parts, axis=-1) * ps_ref[...]

    full = lambda c, rows: jnp.concatenate([proj_out[(c, n)][rows] for n in range(D_MODEL // FILL_COLS)], axis=-1)

    def tail(blk, b):
        rows = row_slices[blk]
        o = (o_ref[rows, :] * (0.5 * g_ref[...])) * (jnp.tanh(0.5 * full(3, rows)) + 1.0)
        a = _dot(o.astype(BF16), wa_ref[...])
        fill()
        merged = _sigmoid(full(5, rows)) * a + _sigmoid(full(6, rows)) * b[rows]
        mix = _dot(merged.astype(BF16), wo_ref[...])
        fill()
        y1_ref[rows, :] = alpha * x[rows] + mix

    ops = prepare(0)
    fill(6)
    recur(0, ops)
    b = pool_branch()
    ops = prepare(1)
    fill(2)
    tail(0, b)
    recur(1, ops)
    fill(len(units))

    mlp = jnp.concatenate([sum(d_parts[(c, m)] for c in range(n_ff)) for m in range(D_MODEL // FILL_COLS)], axis=-1)
    out_ref[...] = _layer_norm(alpha * x1 + mlp, l2g_ref[...], l2b_ref[...])
    tail(1, b)


def _const_spec(shape):
    nd = len(shape)
    return pl.BlockSpec(shape, lambda *_: (0,) * nd, pipeline_mode=pl.Buffered(1))


def _layer(x2d, seq_len, w_in, lb_logits, norm_g, w_a, w_pool, pool_scale, w_out, ln1_g, ln1_b,
           w_up, w_down, ln2_g, ln2_b, alpha):
    N, D = x2d.shape
    tm = TILE_TOKENS
    n_tiles = N // tm
    row = lambda p: p.reshape(1, D).astype(F32)
    return pl.pallas_call(
        functools.partial(_block_kernel, alpha, seq_len // tm),
        grid=(n_tiles + 1,),
        in_specs=[
            pl.BlockSpec((tm, D), lambda t: (jnp.minimum(t, n_tiles - 1), 0)),
            _const_spec((D, IN_COLS)),
            _const_spec(lb_logits.shape),
            _const_spec((1, D)),
            _const_spec((D, D)),
            _const_spec(w_pool.shape),
            _const_spec((1, D)),
            _const_spec((D, D)),
            _const_spec((1, D)),
            _const_spec((1, D)),
            _const_spec((D, D_FF)),
            _const_spec((D_FF, D)),
            _const_spec((1, D)),
            _const_spec((1, D)),
        ],
        out_specs=pl.BlockSpec((tm, D), lambda t: (jnp.maximum(t - 1, 0), 0)),
        out_shape=jax.ShapeDtypeStruct((N, D), F32),
        scratch_shapes=[
            pltpu.VMEM((HEADS // 2, 2 * HEAD_DIM, 2 * HEAD_DIM), F32),
            pltpu.VMEM((POOL_HISTORY, D), F32),
            pltpu.VMEM((tm, D), F32),
            pltpu.VMEM((tm, D), F32),
        ],
        compiler_params=pltpu.CompilerParams(
            dimension_semantics=("arbitrary",),
            vmem_limit_bytes=VMEM_LIMIT_BYTES),
        name="hgrn2_pool_mlp_block",
    )(x2d, w_in.astype(BF16), lb_logits.astype(F32), row(norm_g), w_a.astype(BF16), w_pool.astype(BF16),
      row(pool_scale), w_out.astype(BF16), row(ln1_g), row(ln1_b), w_up.astype(BF16), w_down.astype(BF16),
      row(ln2_g), row(ln2_b))


def kernel(x, w_in, lb_logits, hgrn_norm_g, w_a, w_pool, pool_scale, w_out, ln1_g, ln1_b, w_up, w_down, ln2_g, ln2_b):
    depth = w_in.shape[0]
    assert depth == 1 and lb_logits.shape[0] == depth + 1
    alpha = (2.0 * depth) ** 0.25
    B, S, D = x.shape
    x2d = x.reshape(B * S, D)
    for l in range(depth):
        x2d = _layer(x2d, S, w_in[l], lb_logits, hgrn_norm_g[l], w_a[l], w_pool[l], pool_scale[l], w_out[l],
                     ln1_g[l], ln1_b[l], w_up[l], w_down[l], ln2_g[l], ln2_b[l], alpha)
    return x2d.reshape(B, S, D)
```

```python
import functools

import jax
import jax.numpy as jnp
from jax import lax
from jax.experimental import pallas as pl
from jax.experimental.pallas import tpu as pltpu

D_MODEL = 1024
HEADS = 8
HEAD_DIM = D_MODEL // HEADS
POOL_WINDOWS = (2, 4, 8, 16)
POOL_GROUP_DIM = D_MODEL // len(POOL_WINDOWS)
POOL_HISTORY = 16
D_FF = 4 * D_MODEL
IN_COLS = 7 * D_MODEL
LN_EPS = 1e-5
RMS_EPS = 1e-6

REC_BLOCK = 128
DIAG_CHUNK = 32
TILE_TOKENS = 256
FILL_COLS = 512
FF_CHUNK = 1024
LOAD_ROWS = 512
LOAD_COLS = 1024
VMEM_LIMIT_BYTES = 60 * 1024 * 1024

BF16 = jnp.bfloat16
F32 = jnp.float32


def _sigmoid(z):
    return 0.5 * jnp.tanh(0.5 * z) + 0.5


def _dot(a, b):
    return jnp.dot(a, b, preferred_element_type=F32)


def _dot_nt(a, b):
    return lax.dot_general(a, b, (((1,), (1,)), ((), ())), preferred_element_type=F32)


def _dot_tn(a, b):
    return lax.dot_general(a, b, (((0,), (0,)), ((), ())), preferred_element_type=F32)


def _layer_norm(y, g, b):
    mu = jnp.mean(y, axis=-1, keepdims=True)
    c = y - mu
    var = jnp.mean(c * c, axis=-1, keepdims=True)
    return c * lax.rsqrt(var + LN_EPS) * g + b


def _block_cumsum(tri2, z):
    z1 = z.astype(BF16)
    z2 = (z - z1.astype(F32)).astype(BF16)
    return _dot(tri2, jnp.concatenate([z1, z2], axis=0))


def _row_bcast(g, period, row):
    n = g.shape[0] // period
    g3 = g.reshape(n, period, g.shape[1])
    r = jnp.broadcast_to(g3[:, row:row + 1, :], g3.shape)
    return r.reshape(g.shape)


def _load_weight(src, dst, stage, sem):
    n_rows, n_cols = src.shape
    cols = min(n_cols, LOAD_COLS)
    rows = min(n_rows, LOAD_ROWS)
    n_row_chunks = n_rows // rows
    assert n_rows % rows == 0 and n_cols % cols == 0

    for c in range(n_cols // cols):
        col = pl.ds(c * cols, cols)

        def chunk(i, slot):
            row = pl.ds(pl.multiple_of(i * rows, rows), rows)
            return pltpu.make_async_copy(src.at[row, col], stage.at[slot, pl.ds(0, rows), pl.ds(0, cols)], sem.at[slot])

        chunk(0, 0).start()

        def body(i, carry):
            slot = i % 2

            @pl.when(i + 1 < n_row_chunks)
            def _():
                chunk(i + 1, 1 - slot).start()

            chunk(i, slot).wait()
            row = pl.ds(pl.multiple_of(i * rows, rows), rows)
            dst[row, col] = stage[slot, pl.ds(0, rows), pl.ds(0, cols)].astype(BF16)
            return carry

        lax.fori_loop(0, n_row_chunks, body, 0)


def _block_kernel(alpha, seq_tiles, x_ref, w_in_hbm, lbl_ref, g_ref, wa_hbm, wp_hbm, ps_ref, wo_hbm,
                  l1g_ref, l1b_ref, wu_hbm, wd_hbm, l2g_ref, l2b_ref, out_ref,
                  state_ref, carry_ref, o_ref, y1_ref,
                  w_in_ref, wa_ref, wp_ref, wo_ref, wu_ref, wd_ref, stage_ref, load_sem):
    tm = x_ref.shape[0]
    t = pl.program_id(0)
    j = t % seq_tiles

    @pl.when(t == 0)
    def _():
        y1_ref[...] = jnp.zeros_like(y1_ref)
        for src, dst in ((w_in_hbm, w_in_ref), (wa_hbm, wa_ref), (wp_hbm, wp_ref), (wo_hbm, wo_ref),
                         (wu_hbm, wu_ref), (wd_hbm, wd_ref)):
            _load_weight(src, dst, stage_ref, load_sem)

    @pl.when(j == 0)
    def _():
        state_ref[...] = jnp.zeros_like(state_ref)
        carry_ref[...] = jnp.zeros_like(carry_ref)

    x = x_ref[...]
    xb = x.astype(BF16)

    def proj(c):
        return _dot(xb, w_in_ref[:, c * D_MODEL:(c + 1) * D_MODEL])

    l0 = lbl_ref[0:1, :]
    l1 = lbl_ref[1:2, :]
    lmax = jnp.maximum(l0, l1)
    e0 = jnp.exp(l0 - lmax)
    e1 = jnp.exp(l1 - lmax)
    lb = e0 / (e0 + e1)
    f_mid = 0.5 * (1.0 + lb)
    f_half = 0.5 * (1.0 - lb)

    f_swing = f_half * jnp.tanh(0.5 * proj(1))
    k = f_half - f_swing
    logf = jnp.log(f_mid + f_swing)
    q = proj(0)
    q = (q * (0.5 * HEAD_DIM ** -0.5)) * (jnp.tanh(0.5 * q) + 1.0)
    v = proj(2).astype(BF16)

    L = REC_BLOCK
    assert tm == 2 * L
    PW = 2 * HEAD_DIM
    ti = lax.broadcasted_iota(jnp.int32, (L, 2 * L), 0)
    si = lax.broadcasted_iota(jnp.int32, (L, 2 * L), 1)
    tri2 = ((si % L) <= ti).astype(BF16)
    tp = lax.broadcasted_iota(jnp.int32, (L, PW), 0)
    lp = lax.broadcasted_iota(jnp.int32, (L, PW), 1)
    sp = lp % L
    m_diag = ((tp // DIAG_CHUNK) == (sp // DIAG_CHUNK)) & (sp <= tp)
    m_32 = ((tp // 64) == (sp // 64)) & ((tp % 64) >= 32) & ((sp % 64) < 32)
    m_64 = (tp >= 64) & (sp < 64)
    head_a = lp < HEAD_DIM
    rr = lax.broadcasted_iota(jnp.int32, (PW, PW), 0)
    cc = lax.broadcasted_iota(jnp.int32, (PW, PW), 1)
    same_head = (rr < HEAD_DIM) == (cc < HEAD_DIM)

    def block_diag(pair):
        z = jnp.zeros_like(pair)
        return jnp.concatenate([jnp.where(head_a, pair, z), jnp.where(head_a, z, pair)], axis=0)

    n_blk = tm // L
    row_slices = [slice(i * L, (i + 1) * L) for i in range(n_blk)]
    cums = [_block_cumsum(tri2, logf[rows]) for rows in row_slices]

    def prepare(blk):
        rows = row_slices[blk]
        qb, kb, G = q[rows], k[rows], cums[blk]
        g_last = G[L - 1:L, :]
        ops = dict(
            v=v[rows],
            q_dec=(qb * jnp.exp(G)).astype(BF16),
            k_end=(kb * jnp.exp(g_last - G)).astype(BF16),
            decay=jnp.exp(g_last))
        r_d = _row_bcast(G, DIAG_CHUNK, DIAG_CHUNK // 2 - 1)
        ops["q_d"] = (qb * jnp.exp(G - r_d)).astype(BF16)
        ops["k_d"] = (kb * jnp.exp(r_d - G)).astype(BF16)
        e_32 = jnp.exp(-jnp.abs(G - _row_bcast(G, 64, 31)))
        ops["q_32"] = (qb * e_32).astype(BF16)
        ops["k_32"] = (kb * e_32).astype(BF16)
        e_64 = jnp.exp(-jnp.abs(G - _row_bcast(G, 128, 63)))
        ops["q_64"] = (qb * e_64).astype(BF16)
        ops["k_64"] = (kb * e_64).astype(BF16)
        return ops

    units = []
    proj_out = {}

    def proj_unit(c, n):
        def run():
            lo = c * D_MODEL + n * FILL_COLS
            proj_out[(c, n)] = _dot(xb, w_in_ref[:, lo:lo + FILL_COLS])
        return run

    x1 = _layer_norm(y1_ref[...], l1g_ref[...], l1b_ref[...])
    x1b = x1.astype(BF16)
    n_ff = D_FF // FF_CHUNK
    h_parts, d_parts = {}, {}

    def up_unit(c, n):
        def run():
            lo = c * FF_CHUNK + n * FILL_COLS
            hh = jnp.maximum(_dot(x1b, wu_ref[:, lo:lo + FILL_COLS]), 0.0)
            h_parts[(c, n)] = (hh * hh).astype(BF16)
        return run

    def down_unit(c, m):
        def run():
            hc = jnp.concatenate([h_parts[(c, n)] for n in range(FF_CHUNK // FILL_COLS)], axis=-1)
            d_parts[(c, m)] = _dot(hc, wd_ref[c * FF_CHUNK:(c + 1) * FF_CHUNK, m * FILL_COLS:(m + 1) * FILL_COLS])
        return run

    ups = [[up_unit(c, n) for n in range(FF_CHUNK // FILL_COLS)] for c in range(n_ff)]
    downs = [[down_unit(c, m) for m in range(D_MODEL // FILL_COLS)] for c in range(n_ff)]
    projs = [[proj_unit(c, n) for n in range(D_MODEL // FILL_COLS)] for c in (3, 4, 5, 6)]
    units += projs[0] + projs[1] + projs[2] + projs[3]
    units += ups[0] + ups[1] + downs[0] + ups[2] + downs[1] + ups[3] + downs[2] + downs[3]

    def fill(n=1):
        for _ in range(n):
            if units:
                units.pop(0)()

    def recur(blk, ops):
        rows = row_slices[blk]
        probs = []
        for p in range(HEADS // 2):
            cols = slice(p * PW, (p + 1) * PW)
            s_d = _dot_nt(ops["q_d"][:, cols], block_diag(ops["k_d"][:, cols]))
            s_32 = _dot_nt(ops["q_32"][:, cols], block_diag(ops["k_32"][:, cols]))
            s_64 = _dot_nt(ops["q_64"][:, cols], block_diag(ops["k_64"][:, cols]))
            fill()
            probs.append(jnp.where(m_diag, s_d, jnp.where(m_32, s_32, jnp.where(m_64, s_64, 0.0))).astype(BF16))
        for p in range(HEADS // 2):
            cols = slice(p * PW, (p + 1) * PW)
            state = state_ref[p]
            v_pair = ops["v"][:, cols]
            o_p = _dot(probs[p], block_diag(v_pair)) + _dot(ops["q_dec"][:, cols], state.astype(BF16))
            decay_col = jnp.broadcast_to(ops["decay"][:, cols], (PW, PW)).T
            kv = _dot_tn(ops["k_end"][:, cols], v_pair)
            fill()
            state_ref[p] = decay_col * state + jnp.where(same_head, kv, 0.0)
            for half in range(2):
                o_h = o_p[:, half * HEAD_DIM:(half + 1) * HEAD_DIM]
                o_h = o_h * lax.rsqrt(jnp.mean(o_h * o_h, axis=-1, keepdims=True) + RMS_EPS)
                o_ref[rows, p * PW + half * HEAD_DIM:p * PW + (half + 1) * HEAD_DIM] = o_h

    def pool_branch():
        pv = jnp.concatenate([proj_out[(4, n)] for n in range(D_MODEL // FILL_COLS)], axis=-1)
        ext = jnp.concatenate([carry_ref[...], pv], axis=0)
        carry_ref[...] = pv[tm - POOL_HISTORY:, :]
        pos = j * tm + lax.broadcasted_iota(jnp.int32, (tm, 1), 0)
        b_parts = []
        for gi, w in enumerate(POOL_WINDOWS):
            cols = slice(gi * POOL_GROUP_DIM, (gi + 1) * POOL_GROUP_DIM)
            acc = ext[:, cols]
            span = 1
            while span < w:
                acc = acc + pltpu.roll(acc, span, 0)
                span *= 2
            inv_count = 1.0 / jnp.minimum(pos + 1, w).astype(F32)
            pooled = acc[POOL_HISTORY:, :] * inv_count - pv[:, cols]
            b_parts.append(_dot(pooled.astype(BF16), wp_ref[gi * POOL_GROUP_DIM:(gi + 1) * POOL_GROUP_DIM, :]))
        return jnp.concatenate(b_parts, axis=-1) * ps_ref[...]

    full = lambda c, rows: jnp.concatenate([proj_out[(c, n)][rows] for n in range(D_MODEL // FILL_COLS)], axis=-1)

    def tail(blk, b):
        rows = row_slices[blk]
        o = (o_ref[rows, :] * (0.5 * g_ref[...])) * (jnp.tanh(0.5 * full(3, rows)) + 1.0)
        a = _dot(o.astype(BF16), wa_ref[...])
        fill()
        merged = _sigmoid(full(5, rows)) * a + _sigmoid(full(6, rows)) * b[rows]
        mix = _dot(merged.astype(BF16), wo_ref[...])
        fill()
        y1_ref[rows, :] = alpha * x[rows] + mix

    ops = prepare(0)
    fill(6)
    recur(0, ops)
    b = pool_branch()
    ops = prepare(1)
    fill(2)
    tail(0, b)
    recur(1, ops)
    fill(len(units))

    mlp = jnp.concatenate([sum(d_parts[(c, m)] for c in range(n_ff)) for m in range(D_MODEL // FILL_COLS)], axis=-1)
    out_ref[...] = _layer_norm(alpha * x1 + mlp, l2g_ref[...], l2b_ref[...])
    tail(1, b)


def _const_spec(shape):
    nd = len(shape)
    return pl.BlockSpec(shape, lambda *_: (0,) * nd, pipeline_mode=pl.Buffered(1))


def _layer(x2d, seq_len, w_in, lb_logits, norm_g, w_a, w_pool, pool_scale, w_out, ln1_g, ln1_b,
           w_up, w_down, ln2_g, ln2_b, alpha):
    N, D = x2d.shape
    tm = TILE_TOKENS
    n_tiles = N // tm
    row = lambda p: p.reshape(1, D).astype(F32)
    hbm = pl.BlockSpec(memory_space=pl.ANY)
    return pl.pallas_call(
        functools.partial(_block_kernel, alpha, seq_len // tm),
        grid=(n_tiles + 1,),
        in_specs=[
            pl.BlockSpec((tm, D), lambda t: (jnp.minimum(t, n_tiles - 1), 0)),
            hbm,
            _const_spec(lb_logits.shape),
            _const_spec((1, D)),
            hbm,
            hbm,
            _const_spec((1, D)),
            hbm,
            _const_spec((1, D)),
            _const_spec((1, D)),
            hbm,
            hbm,
            _const_spec((1, D)),
            _const_spec((1, D)),
        ],
        out_specs=pl.BlockSpec((tm, D), lambda t: (jnp.maximum(t - 1, 0), 0)),
        out_shape=jax.ShapeDtypeStruct((N, D), F32),
        scratch_shapes=[
            pltpu.VMEM((HEADS // 2, 2 * HEAD_DIM, 2 * HEAD_DIM), F32),
            pltpu.VMEM((POOL_HISTORY, D), F32),
            pltpu.VMEM((tm, D), F32),
            pltpu.VMEM((tm, D), F32),
            pltpu.VMEM((D, IN_COLS), BF16),
            pltpu.VMEM((D, D), BF16),
            pltpu.VMEM((D, POOL_GROUP_DIM), BF16),
            pltpu.VMEM((D, D), BF16),
            pltpu.VMEM((D, D_FF), BF16),
            pltpu.VMEM((D_FF, D), BF16),
            pltpu.VMEM((2, LOAD_ROWS, LOAD_COLS), F32),
            pltpu.SemaphoreType.DMA((2,)),
        ],
        compiler_params=pltpu.CompilerParams(
            dimension_semantics=("arbitrary",),
            vmem_limit_bytes=VMEM_LIMIT_BYTES),
        name="hgrn2_pool_mlp_block",
    )(x2d, w_in, lb_logits.astype(F32), row(norm_g), w_a, w_pool.reshape(D, POOL_GROUP_DIM),
      row(pool_scale), w_out, row(ln1_g), row(ln1_b), w_up, w_down, row(ln2_g), row(ln2_b))


def kernel(x, w_in, lb_logits, hgrn_norm_g, w_a, w_pool, pool_scale, w_out, ln1_g, ln1_b, w_up, w_down, ln2_g, ln2_b):
    depth = w_in.shape[0]
    assert depth == 1 and lb_logits.shape[0] == depth + 1
    alpha = (2.0 * depth) ** 0.25
    B, S, D = x.shape
    x2d = x.reshape(B * S, D)
    for l in range(depth):
        x2d = _layer(x2d, S, w_in[l], lb_logits, hgrn_norm_g[l], w_a[l], w_pool[l], pool_scale[l], w_out[l],
                     ln1_g[l], ln1_b[l], w_up[l], w_down[l], ln2_g[l], ln2_b[l], alpha)
    return x2d.reshape(B, S, D)
```

```python
import functools

import jax
import jax.numpy as jnp
from jax import lax
from jax.experimental import pallas as pl
from jax.experimental.pallas import tpu as pltpu

D_MODEL = 1024
HEADS = 8
HEAD_DIM = D_MODEL // HEADS
POOL_WINDOWS = (2, 4, 8, 16)
POOL_GROUP_DIM = D_MODEL // len(POOL_WINDOWS)
POOL_HISTORY = 16
D_FF = 4 * D_MODEL
IN_COLS = 7 * D_MODEL
LN_EPS = 1e-5
RMS_EPS = 1e-6

REC_BLOCK = 128
DIAG_CHUNK = 32
TILE_TOKENS = 256
FILL_COLS = 512
FF_CHUNK = 1024
LOAD_ROWS = 256
LOAD_SLOTS = 4
VMEM_LIMIT_BYTES = 60 * 1024 * 1024

BF16 = jnp.bfloat16
F32 = jnp.float32


def _sigmoid(z):
    return 0.5 * jnp.tanh(0.5 * z) + 0.5


def _dot(a, b):
    return jnp.dot(a, b, preferred_element_type=F32)


def _dot_nt(a, b):
    return lax.dot_general(a, b, (((1,), (1,)), ((), ())), preferred_element_type=F32)


def _dot_tn(a, b):
    return lax.dot_general(a, b, (((0,), (0,)), ((), ())), preferred_element_type=F32)


def _layer_norm(y, g, b):
    mu = jnp.mean(y, axis=-1, keepdims=True)
    c = y - mu
    var = jnp.mean(c * c, axis=-1, keepdims=True)
    return c * lax.rsqrt(var + LN_EPS) * g + b


def _block_cumsum(tri2, z):
    z1 = z.astype(BF16)
    z2 = (z - z1.astype(F32)).astype(BF16)
    return _dot(tri2, jnp.concatenate([z1, z2], axis=0))


def _row_bcast(g, period, row):
    n = g.shape[0] // period
    g3 = g.reshape(n, period, g.shape[1])
    r = jnp.broadcast_to(g3[:, row:row + 1, :], g3.shape)
    return r.reshape(g.shape)


def _load_weight(src_chunk, dst, stage, sem):
    n_blocks, n_rows, cols = dst.shape
    per_block = n_rows // LOAD_ROWS
    n = n_blocks * per_block
    ahead = LOAD_SLOTS - 1
    assert n_rows % LOAD_ROWS == 0 and n >= ahead

    def split(i):
        return i // per_block, pl.multiple_of((i % per_block) * LOAD_ROWS, LOAD_ROWS)

    def copy(i):
        blk, r0 = split(i)
        slot = i % LOAD_SLOTS
        return pltpu.make_async_copy(src_chunk(blk, r0), stage.at[slot, :, pl.ds(0, cols)], sem.at[slot])

    for i in range(ahead):
        copy(i).start()

    def body(i, carry):
        @pl.when(i + ahead < n)
        def _():
            copy(i + ahead).start()

        copy(i).wait()
        blk, r0 = split(i)
        dst[blk, pl.ds(r0, LOAD_ROWS), :] = stage[i % LOAD_SLOTS, :, pl.ds(0, cols)].astype(BF16)
        return carry

    lax.fori_loop(0, n, body, 0)


def _block_kernel(alpha, seq_tiles, x_ref, w_in_hbm, lbl_ref, g_ref, wa_hbm, wp_hbm, ps_ref, wo_hbm,
                  l1g_ref, l1b_ref, wu_hbm, wd_hbm, l2g_ref, l2b_ref, out_ref,
                  state_ref, carry_ref, o_ref, y1_ref,
                  w_in_ref, wa_ref, wp_ref, wo_ref, wu_ref, wd_ref, stage_ref, load_sem):
    tm = x_ref.shape[0]
    t = pl.program_id(0)
    j = t % seq_tiles

    @pl.when(t == 0)
    def _():
        y1_ref[...] = jnp.zeros_like(y1_ref)
        rows = lambda r0: pl.ds(r0, LOAD_ROWS)
        col_block = lambda hbm: (lambda b, r0: hbm.at[rows(r0), pl.ds(pl.multiple_of(b * D_MODEL, D_MODEL), D_MODEL)])
        row_block = lambda hbm: (lambda b, r0: hbm.at[rows(pl.multiple_of(b * D_MODEL, D_MODEL) + r0), :])
        _load_weight(col_block(w_in_hbm), w_in_ref, stage_ref, load_sem)
        _load_weight(row_block(wa_hbm), wa_ref, stage_ref, load_sem)
        _load_weight(row_block(wp_hbm), wp_ref, stage_ref, load_sem)
        _load_weight(row_block(wo_hbm), wo_ref, stage_ref, load_sem)
        _load_weight(col_block(wu_hbm), wu_ref, stage_ref, load_sem)
        _load_weight(row_block(wd_hbm), wd_ref, stage_ref, load_sem)

    @pl.when(j == 0)
    def _():
        state_ref[...] = jnp.zeros_like(state_ref)
        carry_ref[...] = jnp.zeros_like(carry_ref)

    x = x_ref[...]
    xb = x.astype(BF16)

    def proj(c):
        return _dot(xb, w_in_ref[c])

    l0 = lbl_ref[0:1, :]
    l1 = lbl_ref[1:2, :]
    lmax = jnp.maximum(l0, l1)
    e0 = jnp.exp(l0 - lmax)
    e1 = jnp.exp(l1 - lmax)
    lb = e0 / (e0 + e1)
    f_mid = 0.5 * (1.0 + lb)
    f_half = 0.5 * (1.0 - lb)

    f_swing = f_half * jnp.tanh(0.5 * proj(1))
    k = f_half - f_swing
    logf = jnp.log(f_mid + f_swing)
    q = proj(0)
    q = (q * (0.5 * HEAD_DIM ** -0.5)) * (jnp.tanh(0.5 * q) + 1.0)
    v = proj(2).astype(BF16)

    L = REC_BLOCK
    assert tm == 2 * L
    PW = 2 * HEAD_DIM
    ti = lax.broadcasted_iota(jnp.int32, (L, 2 * L), 0)
    si = lax.broadcasted_iota(jnp.int32, (L, 2 * L), 1)
    tri2 = ((si % L) <= ti).astype(BF16)
    tp = lax.broadcasted_iota(jnp.int32, (L, PW), 0)
    lp = lax.broadcasted_iota(jnp.int32, (L, PW), 1)
    sp = lp % L
    m_diag = ((tp // DIAG_CHUNK) == (sp // DIAG_CHUNK)) & (sp <= tp)
    m_32 = ((tp // 64) == (sp // 64)) & ((tp % 64) >= 32) & ((sp % 64) < 32)
    m_64 = (tp >= 64) & (sp < 64)
    head_a = lp < HEAD_DIM
    rr = lax.broadcasted_iota(jnp.int32, (PW, PW), 0)
    cc = lax.broadcasted_iota(jnp.int32, (PW, PW), 1)
    same_head = (rr < HEAD_DIM) == (cc < HEAD_DIM)

    def block_diag(pair):
        z = jnp.zeros_like(pair)
        return jnp.concatenate([jnp.where(head_a, pair, z), jnp.where(head_a, z, pair)], axis=0)

    n_blk = tm // L
    row_slices = [slice(i * L, (i + 1) * L) for i in range(n_blk)]
    cums = [_block_cumsum(tri2, logf[rows]) for rows in row_slices]

    def prepare(blk):
        rows = row_slices[blk]
        qb, kb, G = q[rows], k[rows], cums[blk]
        g_last = G[L - 1:L, :]
        ops = dict(
            v=v[rows],
            q_dec=(qb * jnp.exp(G)).astype(BF16),
            k_end=(kb * jnp.exp(g_last - G)).astype(BF16),
            decay=jnp.exp(g_last))
        r_d = _row_bcast(G, DIAG_CHUNK, DIAG_CHUNK // 2 - 1)
        ops["q_d"] = (qb * jnp.exp(G - r_d)).astype(BF16)
        ops["k_d"] = (kb * jnp.exp(r_d - G)).astype(BF16)
        e_32 = jnp.exp(-jnp.abs(G - _row_bcast(G, 64, 31)))
        ops["q_32"] = (qb * e_32).astype(BF16)
        ops["k_32"] = (kb * e_32).astype(BF16)
        e_64 = jnp.exp(-jnp.abs(G - _row_bcast(G, 128, 63)))
        ops["q_64"] = (qb * e_64).astype(BF16)
        ops["k_64"] = (kb * e_64).astype(BF16)
        return ops

    units = []
    proj_out = {}

    def proj_unit(c, n):
        def run():
            proj_out[(c, n)] = _dot(xb, w_in_ref[c, :, n * FILL_COLS:(n + 1) * FILL_COLS])
        return run

    x1 = _layer_norm(y1_ref[...], l1g_ref[...], l1b_ref[...])
    x1b = x1.astype(BF16)
    n_ff = D_FF // FF_CHUNK
    assert FF_CHUNK == D_MODEL
    h_parts, d_parts = {}, {}

    def up_unit(c, n):
        def run():
            hh = jnp.maximum(_dot(x1b, wu_ref[c, :, n * FILL_COLS:(n + 1) * FILL_COLS]), 0.0)
            h_parts[(c, n)] = (hh * hh).astype(BF16)
        return run

    def down_unit(c, m):
        def run():
            hc = jnp.concatenate([h_parts[(c, n)] for n in range(FF_CHUNK // FILL_COLS)], axis=-1)
            d_parts[(c, m)] = _dot(hc, wd_ref[c, :, m * FILL_COLS:(m + 1) * FILL_COLS])
        return run

    ups = [[up_unit(c, n) for n in range(FF_CHUNK // FILL_COLS)] for c in range(n_ff)]
    downs = [[down_unit(c, m) for m in range(D_MODEL // FILL_COLS)] for c in range(n_ff)]
    projs = [[proj_unit(c, n) for n in range(D_MODEL // FILL_COLS)] for c in (3, 4, 5, 6)]
    units += projs[0] + projs[1] + projs[2] + projs[3]
    units += ups[0] + ups[1] + downs[0] + ups[2] + downs[1] + ups[3] + downs[2] + downs[3]

    def fill(n=1):
        for _ in range(n):
            if units:
                units.pop(0)()

    def recur(blk, ops):
        rows = row_slices[blk]
        probs = []
        for p in range(HEADS // 2):
            cols = slice(p * PW, (p + 1) * PW)
            s_d = _dot_nt(ops["q_d"][:, cols], block_diag(ops["k_d"][:, cols]))
            s_32 = _dot_nt(ops["q_32"][:, cols], block_diag(ops["k_32"][:, cols]))
            s_64 = _dot_nt(ops["q_64"][:, cols], block_diag(ops["k_64"][:, cols]))
            fill()
            probs.append(jnp.where(m_diag, s_d, jnp.where(m_32, s_32, jnp.where(m_64, s_64, 0.0))).astype(BF16))
        for p in range(HEADS // 2):
            cols = slice(p * PW, (p + 1) * PW)
            state = state_ref[p]
            v_pair = ops["v"][:, cols]
            o_p = _dot(probs[p], block_diag(v_pair)) + _dot(ops["q_dec"][:, cols], state.astype(BF16))
            decay_col = jnp.broadcast_to(ops["decay"][:, cols], (PW, PW)).T
            kv = _dot_tn(ops["k_end"][:, cols], v_pair)
            fill()
            state_ref[p] = decay_col * state + jnp.where(same_head, kv, 0.0)
            for half in range(2):
                o_h = o_p[:, half * HEAD_DIM:(half + 1) * HEAD_DIM]
                o_h = o_h * lax.rsqrt(jnp.mean(o_h * o_h, axis=-1, keepdims=True) + RMS_EPS)
                o_ref[rows, p * PW + half * HEAD_DIM:p * PW + (half + 1) * HEAD_DIM] = o_h

    def pool_branch():
        pv = jnp.concatenate([proj_out[(4, n)] for n in range(D_MODEL // FILL_COLS)], axis=-1)
        ext = jnp.concatenate([carry_ref[...], pv], axis=0)
        carry_ref[...] = pv[tm - POOL_HISTORY:, :]
        pos = j * tm + lax.broadcasted_iota(jnp.int32, (tm, 1), 0)
        b_parts = []
        for gi, w in enumerate(POOL_WINDOWS):
            cols = slice(gi * POOL_GROUP_DIM, (gi + 1) * POOL_GROUP_DIM)
            acc = ext[:, cols]
            span = 1
            while span < w:
                acc = acc + pltpu.roll(acc, span, 0)
                span *= 2
            inv_count = 1.0 / jnp.minimum(pos + 1, w).astype(F32)
            pooled = acc[POOL_HISTORY:, :] * inv_count - pv[:, cols]
            b_parts.append(_dot(pooled.astype(BF16), wp_ref[0, gi * POOL_GROUP_DIM:(gi + 1) * POOL_GROUP_DIM, :]))
        return jnp.concatenate(b_parts, axis=-1) * ps_ref[...]

    full = lambda c, rows: jnp.concatenate([proj_out[(c, n)][rows] for n in range(D_MODEL // FILL_COLS)], axis=-1)

    def tail(blk, b):
        rows = row_slices[blk]
        o = (o_ref[rows, :] * (0.5 * g_ref[...])) * (jnp.tanh(0.5 * full(3, rows)) + 1.0)
        a = _dot(o.astype(BF16), wa_ref[0])
        fill()
        merged = _sigmoid(full(5, rows)) * a + _sigmoid(full(6, rows)) * b[rows]
        mix = _dot(merged.astype(BF16), wo_ref[0])
        fill()
        y1_ref[rows, :] = alpha * x[rows] + mix

    ops = prepare(0)
    fill(6)
    recur(0, ops)
    b = pool_branch()
    ops = prepare(1)
    fill(2)
    tail(0, b)
    recur(1, ops)
    fill(len(units))

    mlp = jnp.concatenate([sum(d_parts[(c, m)] for c in range(n_ff)) for m in range(D_MODEL // FILL_COLS)], axis=-1)
    out_ref[...] = _layer_norm(alpha * x1 + mlp, l2g_ref[...], l2b_ref[...])
    tail(1, b)


def _const_spec(shape):
    nd = len(shape)
    return pl.BlockSpec(shape, lambda *_: (0,) * nd, pipeline_mode=pl.Buffered(1))


def _layer(x2d, seq_len, w_in, lb_logits, norm_g, w_a, w_pool, pool_scale, w_out, ln1_g, ln1_b,
           w_up, w_down, ln2_g, ln2_b, alpha):
    N, D = x2d.shape
    tm = TILE_TOKENS
    n_tiles = N // tm
    row = lambda p: p.reshape(1, D).astype(F32)
    hbm = pl.BlockSpec(memory_space=pl.ANY)
    return pl.pallas_call(
        functools.partial(_block_kernel, alpha, seq_len // tm),
        grid=(n_tiles + 1,),
        in_specs=[
            pl.BlockSpec((tm, D), lambda t: (jnp.minimum(t, n_tiles - 1), 0)),
            hbm,
            _const_spec(lb_logits.shape),
            _const_spec((1, D)),
            hbm,
            hbm,
            _const_spec((1, D)),
            hbm,
            _const_spec((1, D)),
            _const_spec((1, D)),
            hbm,
            hbm,
            _const_spec((1, D)),
            _const_spec((1, D)),
        ],
        out_specs=pl.BlockSpec((tm, D), lambda t: (jnp.maximum(t - 1, 0), 0)),
        out_shape=jax.ShapeDtypeStruct((N, D), F32),
        scratch_shapes=[
            pltpu.VMEM((HEADS // 2, 2 * HEAD_DIM, 2 * HEAD_DIM), F32),
            pltpu.VMEM((POOL_HISTORY, D), F32),
            pltpu.VMEM((tm, D), F32),
            pltpu.VMEM((tm, D), F32),
            pltpu.VMEM((IN_COLS // D, D, D), BF16),
            pltpu.VMEM((1, D, D), BF16),
            pltpu.VMEM((1, D, POOL_GROUP_DIM), BF16),
            pltpu.VMEM((1, D, D), BF16),
            pltpu.VMEM((D_FF // D, D, D), BF16),
            pltpu.VMEM((D_FF // D, D, D), BF16),
            pltpu.VMEM((LOAD_SLOTS, LOAD_ROWS, D), F32),
            pltpu.SemaphoreType.DMA((LOAD_SLOTS,)),
        ],
        compiler_params=pltpu.CompilerParams(
            dimension_semantics=("arbitrary",),
            vmem_limit_bytes=VMEM_LIMIT_BYTES),
        name="hgrn2_pool_mlp_block",
    )(x2d, w_in, lb_logits.astype(F32), row(norm_g), w_a, w_pool.reshape(D, POOL_GROUP_DIM),
      row(pool_scale), w_out, row(ln1_g), row(ln1_b), w_up, w_down, row(ln2_g), row(ln2_b))


def kernel(x, w_in, lb_logits, hgrn_norm_g, w_a, w_pool, pool_scale, w_out, ln1_g, ln1_b, w_up, w_down, ln2_g, ln2_b):
    depth = w_in.shape[0]
    assert depth == 1 and lb_logits.shape[0] == depth + 1
    alpha = (2.0 * depth) ** 0.25
    B, S, D = x.shape
    x2d = x.reshape(B * S, D)
    for l in range(depth):
        x2d = _layer(x2d, S, w_in[l], lb_logits, hgrn_norm_g[l], w_a[l], w_pool[l], pool_scale[l], w_out[l],
                     ln1_g[l], ln1_b[l], w_up[l], w_down[l], ln2_g[l], ln2_b[l], alpha)
    return x2d.reshape(B, S, D)
```

```python
import functools

import jax
import jax.numpy as jnp
from jax import lax
from jax.experimental import pallas as pl
from jax.experimental.pallas import tpu as pltpu

D_MODEL = 1024
HEADS = 8
HEAD_DIM = D_MODEL // HEADS
POOL_WINDOWS = (2, 4, 8, 16)
POOL_GROUP_DIM = D_MODEL // len(POOL_WINDOWS)
POOL_HISTORY = 16
D_FF = 4 * D_MODEL
IN_COLS = 7 * D_MODEL
LN_EPS = 1e-5
RMS_EPS = 1e-6

REC_BLOCK = 128
DIAG_CHUNK = 32
TILE_TOKENS = 256
FILL_COLS = 512
FF_CHUNK = 1024
LOAD_ROWS = 256
LOAD_SLOTS = 4
VMEM_LIMIT_BYTES = 60 * 1024 * 1024

BF16 = jnp.bfloat16
F32 = jnp.float32


def _sigmoid(z):
    return 0.5 * jnp.tanh(0.5 * z) + 0.5


def _dot(a, b):
    return jnp.dot(a, b, preferred_element_type=F32)


def _dot_nt(a, b):
    return lax.dot_general(a, b, (((1,), (1,)), ((), ())), preferred_element_type=F32)


def _dot_tn(a, b):
    return lax.dot_general(a, b, (((0,), (0,)), ((), ())), preferred_element_type=F32)


def _layer_norm(y, g, b):
    mu = jnp.mean(y, axis=-1, keepdims=True)
    c = y - mu
    var = jnp.mean(c * c, axis=-1, keepdims=True)
    return c * lax.rsqrt(var + LN_EPS) * g + b


def _block_cumsum(tri2, z):
    z1 = z.astype(BF16)
    z2 = (z - z1.astype(F32)).astype(BF16)
    return _dot(tri2, jnp.concatenate([z1, z2], axis=0))


def _row_bcast(g, period, row):
    n = g.shape[0] // period
    g3 = g.reshape(n, period, g.shape[1])
    r = jnp.broadcast_to(g3[:, row:row + 1, :], g3.shape)
    return r.reshape(g.shape)


def _load_weight(src_chunk, dst, stage, sem):
    n_blocks, n_rows, cols = dst.shape
    per_block = n_rows // LOAD_ROWS
    n = n_blocks * per_block
    ahead = LOAD_SLOTS - 1
    assert n_rows % LOAD_ROWS == 0 and n >= ahead

    def split(i):
        return i // per_block, pl.multiple_of((i % per_block) * LOAD_ROWS, LOAD_ROWS)

    def copy(i):
        blk, r0 = split(i)
        slot = i % LOAD_SLOTS
        return pltpu.make_async_copy(src_chunk(blk, r0), stage.at[slot, :, pl.ds(0, cols)], sem.at[slot])

    for i in range(ahead):
        copy(i).start()

    def body(i, carry):
        @pl.when(i + ahead < n)
        def _():
            copy(i + ahead).start()

        copy(i).wait()
        blk, r0 = split(i)
        dst[blk, pl.ds(r0, LOAD_ROWS), :] = stage[i % LOAD_SLOTS, :, pl.ds(0, cols)].astype(BF16)
        return carry

    lax.fori_loop(0, n, body, 0)


def _block_kernel(alpha, seq_tiles, x_ref, w_in_hbm, lbl_ref, g_ref, wa_hbm, wp_hbm, ps_ref, wo_hbm,
                  l1g_ref, l1b_ref, wu_hbm, wd_hbm, l2g_ref, l2b_ref, out_ref,
                  state_ref, carry_ref, o_ref, y1_ref,
                  w_in_ref, wa_ref, wp_ref, wo_ref, wu_ref, wd_ref, stage_ref, load_sem):
    tm = x_ref.shape[0]
    t = pl.program_id(0)
    j = t % seq_tiles

    @pl.when(t == 0)
    def _():
        y1_ref[...] = jnp.zeros_like(y1_ref)
        rows = lambda r0: pl.ds(r0, LOAD_ROWS)
        col_block = lambda hbm: (lambda b, r0: hbm.at[rows(r0), pl.ds(pl.multiple_of(b * D_MODEL, D_MODEL), D_MODEL)])
        row_block = lambda hbm: (lambda b, r0: hbm.at[rows(pl.multiple_of(b * D_MODEL, D_MODEL) + r0), :])
        _load_weight(col_block(w_in_hbm), w_in_ref, stage_ref, load_sem)
        _load_weight(row_block(wa_hbm), wa_ref, stage_ref, load_sem)
        _load_weight(row_block(wp_hbm), wp_ref, stage_ref, load_sem)
        _load_weight(row_block(wo_hbm), wo_ref, stage_ref, load_sem)
        _load_weight(col_block(wu_hbm), wu_ref, stage_ref, load_sem)
        _load_weight(row_block(wd_hbm), wd_ref, stage_ref, load_sem)

    @pl.when(j == 0)
    def _():
        state_ref[...] = jnp.zeros_like(state_ref)
        carry_ref[...] = jnp.zeros_like(carry_ref)

    x = x_ref[...]
    xb = x.astype(BF16)

    def proj(c):
        return _dot(xb, w_in_ref[c])

    l0 = lbl_ref[0:1, :]
    l1 = lbl_ref[1:2, :]
    lmax = jnp.maximum(l0, l1)
    e0 = jnp.exp(l0 - lmax)
    e1 = jnp.exp(l1 - lmax)
    lb = e0 / (e0 + e1)
    f_mid = 0.5 * (1.0 + lb)
    f_half = 0.5 * (1.0 - lb)

    f_swing = f_half * jnp.tanh(0.5 * proj(1))
    k = f_half - f_swing
    logf = jnp.log(f_mid + f_swing)
    q = proj(0)
    q = (q * (0.5 * HEAD_DIM ** -0.5)) * (jnp.tanh(0.5 * q) + 1.0)
    v = proj(2).astype(BF16)

    L = REC_BLOCK
    assert tm == 2 * L
    PW = 2 * HEAD_DIM
    ti = lax.broadcasted_iota(jnp.int32, (L, 2 * L), 0)
    si = lax.broadcasted_iota(jnp.int32, (L, 2 * L), 1)
    tri2 = ((si % L) <= ti).astype(BF16)
    tp = lax.broadcasted_iota(jnp.int32, (L, PW), 0)
    lp = lax.broadcasted_iota(jnp.int32, (L, PW), 1)
    sp = lp % L
    m_diag = ((tp // DIAG_CHUNK) == (sp // DIAG_CHUNK)) & (sp <= tp)
    m_32 = ((tp // 64) == (sp // 64)) & ((tp % 64) >= 32) & ((sp % 64) < 32)
    m_64 = (tp >= 64) & (sp < 64)
    head_a = lp < HEAD_DIM

    def block_diag(pair):
        z = jnp.zeros_like(pair)
        return jnp.concatenate([jnp.where(head_a, pair, z), jnp.where(head_a, z, pair)], axis=0)

    n_blk = tm // L
    row_slices = [slice(i * L, (i + 1) * L) for i in range(n_blk)]
    cums = [_block_cumsum(tri2, logf[rows]) for rows in row_slices]

    def prepare(blk):
        rows = row_slices[blk]
        qb, kb, G = q[rows], k[rows], cums[blk]
        g_last = G[L - 1:L, :]
        ops = dict(
            v=v[rows],
            q_dec=(qb * jnp.exp(G)).astype(BF16),
            k_end=(kb * jnp.exp(g_last - G)).astype(BF16),
            decay=jnp.exp(g_last))
        r_d = _row_bcast(G, DIAG_CHUNK, DIAG_CHUNK // 2 - 1)
        ops["q_d"] = (qb * jnp.exp(G - r_d)).astype(BF16)
        ops["k_d"] = (kb * jnp.exp(r_d - G)).astype(BF16)
        e_32 = jnp.exp(-jnp.abs(G - _row_bcast(G, 64, 31)))
        ops["q_32"] = (qb * e_32).astype(BF16)
        ops["k_32"] = (kb * e_32).astype(BF16)
        e_64 = jnp.exp(-jnp.abs(G - _row_bcast(G, 128, 63)))
        ops["q_64"] = (qb * e_64).astype(BF16)
        ops["k_64"] = (kb * e_64).astype(BF16)
        return ops

    units = []
    proj_out = {}

    def proj_unit(c, n):
        def run():
            proj_out[(c, n)] = _dot(xb, w_in_ref[c, :, n * FILL_COLS:(n + 1) * FILL_COLS])
        return run

    x1 = _layer_norm(y1_ref[...], l1g_ref[...], l1b_ref[...])
    x1b = x1.astype(BF16)
    n_ff = D_FF // FF_CHUNK
    assert FF_CHUNK == D_MODEL
    h_parts, d_parts = {}, {}

    def up_unit(c, n):
        def run():
            hh = jnp.maximum(_dot(x1b, wu_ref[c, :, n * FILL_COLS:(n + 1) * FILL_COLS]), 0.0)
            h_parts[(c, n)] = (hh * hh).astype(BF16)
        return run

    def down_unit(c, m):
        def run():
            hc = jnp.concatenate([h_parts[(c, n)] for n in range(FF_CHUNK // FILL_COLS)], axis=-1)
            d_parts[(c, m)] = _dot(hc, wd_ref[c, :, m * FILL_COLS:(m + 1) * FILL_COLS])
        return run

    ups = [[up_unit(c, n) for n in range(FF_CHUNK // FILL_COLS)] for c in range(n_ff)]
    downs = [[down_unit(c, m) for m in range(D_MODEL // FILL_COLS)] for c in range(n_ff)]
    projs = [[proj_unit(c, n) for n in range(D_MODEL // FILL_COLS)] for c in (3, 4, 5, 6)]
    units += projs[0] + projs[1] + projs[2] + projs[3]
    units += ups[0] + ups[1] + downs[0] + ups[2] + downs[1] + ups[3] + downs[2] + downs[3]

    def fill(n=1):
        for _ in range(n):
            if units:
                units.pop(0)()

    def recur(blk, ops):
        rows = row_slices[blk]
        probs = []
        for p in range(HEADS // 2):
            cols = slice(p * PW, (p + 1) * PW)
            s_d = _dot_nt(ops["q_d"][:, cols], block_diag(ops["k_d"][:, cols]))
            s_32 = _dot_nt(ops["q_32"][:, cols], block_diag(ops["k_32"][:, cols]))
            s_64 = _dot_nt(ops["q_64"][:, cols], block_diag(ops["k_64"][:, cols]))
            fill()
            probs.append(jnp.where(m_diag, s_d, jnp.where(m_32, s_32, jnp.where(m_64, s_64, 0.0))).astype(BF16))
        for p in range(HEADS // 2):
            cols = slice(p * PW, (p + 1) * PW)
            state = state_ref[p]
            v_bd = block_diag(ops["v"][:, cols])
            o_p = _dot(probs[p], v_bd) + _dot(ops["q_dec"][:, cols], block_diag(state.astype(BF16)))
            k_pair = ops["k_end"][:, cols]
            k_rows = jnp.concatenate([k_pair[:, :HEAD_DIM], k_pair[:, HEAD_DIM:]], axis=0)
            kv = _dot_tn(k_rows, v_bd)
            dec = ops["decay"][:, cols]
            decay_col = jnp.concatenate(
                [jnp.broadcast_to(dec[:, h * HEAD_DIM:(h + 1) * HEAD_DIM], (HEAD_DIM, HEAD_DIM)).T for h in range(2)], axis=1)
            fill()
            state_ref[p] = decay_col * state + kv
            for half in range(2):
                o_h = o_p[:, half * HEAD_DIM:(half + 1) * HEAD_DIM]
                o_h = o_h * lax.rsqrt(jnp.mean(o_h * o_h, axis=-1, keepdims=True) + RMS_EPS)
                o_ref[rows, p * PW + half * HEAD_DIM:p * PW + (half + 1) * HEAD_DIM] = o_h

    def pool_branch():
        pv = jnp.concatenate([proj_out[(4, n)] for n in range(D_MODEL // FILL_COLS)], axis=-1)
        ext = jnp.concatenate([carry_ref[...], pv], axis=0)
        carry_ref[...] = pv[tm - POOL_HISTORY:, :]
        pos = j * tm + lax.broadcasted_iota(jnp.int32, (tm, 1), 0)
        b_parts = []
        for gi, w in enumerate(POOL_WINDOWS):
            cols = slice(gi * POOL_GROUP_DIM, (gi + 1) * POOL_GROUP_DIM)
            acc = ext[:, cols]
            span = 1
            while span < w:
                acc = acc + pltpu.roll(acc, span, 0)
                span *= 2
            inv_count = 1.0 / jnp.minimum(pos + 1, w).astype(F32)
            pooled = acc[POOL_HISTORY:, :] * inv_count - pv[:, cols]
            b_parts.append(_dot(pooled.astype(BF16), wp_ref[0, gi * POOL_GROUP_DIM:(gi + 1) * POOL_GROUP_DIM, :]))
        return jnp.concatenate(b_parts, axis=-1) * ps_ref[...]

    full = lambda c, rows: jnp.concatenate([proj_out[(c, n)][rows] for n in range(D_MODEL // FILL_COLS)], axis=-1)

    def tail(blk, b):
        rows = row_slices[blk]
        o = (o_ref[rows, :] * (0.5 * g_ref[...])) * (jnp.tanh(0.5 * full(3, rows)) + 1.0)
        a = _dot(o.astype(BF16), wa_ref[0])
        fill()
        merged = _sigmoid(full(5, rows)) * a + _sigmoid(full(6, rows)) * b[rows]
        mix = _dot(merged.astype(BF16), wo_ref[0])
        fill()
        y1_ref[rows, :] = alpha * x[rows] + mix

    ops = prepare(0)
    fill(6)
    recur(0, ops)
    b = pool_branch()
    ops = prepare(1)
    fill(2)
    tail(0, b)
    recur(1, ops)
    fill(len(units))

    mlp = jnp.concatenate([sum(d_parts[(c, m)] for c in range(n_ff)) for m in range(D_MODEL // FILL_COLS)], axis=-1)
    out_ref[...] = _layer_norm(alpha * x1 + mlp, l2g_ref[...], l2b_ref[...])
    tail(1, b)


def _const_spec(shape):
    nd = len(shape)
    return pl.BlockSpec(shape, lambda *_: (0,) * nd, pipeline_mode=pl.Buffered(1))


def _layer(x2d, seq_len, w_in, lb_logits, norm_g, w_a, w_pool, pool_scale, w_out, ln1_g, ln1_b,
           w_up, w_down, ln2_g, ln2_b, alpha):
    N, D = x2d.shape
    tm = TILE_TOKENS
    n_tiles = N // tm
    row = lambda p: p.reshape(1, D).astype(F32)
    hbm = pl.BlockSpec(memory_space=pl.ANY)
    return pl.pallas_call(
        functools.partial(_block_kernel, alpha, seq_len // tm),
        grid=(n_tiles + 1,),
        in_specs=[
            pl.BlockSpec((tm, D), lambda t: (jnp.minimum(t, n_tiles - 1), 0)),
            hbm,
            _const_spec(lb_logits.shape),
            _const_spec((1, D)),
            hbm,
            hbm,
            _const_spec((1, D)),
            hbm,
            _const_spec((1, D)),
            _const_spec((1, D)),
            hbm,
            hbm,
            _const_spec((1, D)),
            _const_spec((1, D)),
        ],
        out_specs=pl.BlockSpec((tm, D), lambda t: (jnp.maximum(t - 1, 0), 0)),
        out_shape=jax.ShapeDtypeStruct((N, D), F32),
        scratch_shapes=[
            pltpu.VMEM((HEADS // 2, HEAD_DIM, 2 * HEAD_DIM), F32),
            pltpu.VMEM((POOL_HISTORY, D), F32),
            pltpu.VMEM((tm, D), F32),
            pltpu.VMEM((tm, D), F32),
            pltpu.VMEM((IN_COLS // D, D, D), BF16),
            pltpu.VMEM((1, D, D), BF16),
            pltpu.VMEM((1, D, POOL_GROUP_DIM), BF16),
            pltpu.VMEM((1, D, D), BF16),
            pltpu.VMEM((D_FF // D, D, D), BF16),
            pltpu.VMEM((D_FF // D, D, D), BF16),
            pltpu.VMEM((LOAD_SLOTS, LOAD_ROWS, D), F32),
            pltpu.SemaphoreType.DMA((LOAD_SLOTS,)),
        ],
        compiler_params=pltpu.CompilerParams(
            dimension_semantics=("arbitrary",),
            vmem_limit_bytes=VMEM_LIMIT_BYTES),
        name="hgrn2_pool_mlp_block",
    )(x2d, w_in, lb_logits.astype(F32), row(norm_g), w_a, w_pool.reshape(D, POOL_GROUP_DIM),
      row(pool_scale), w_out, row(ln1_g), row(ln1_b), w_up, w_down, row(ln2_g), row(ln2_b))


def kernel(x, w_in, lb_logits, hgrn_norm_g, w_a, w_pool, pool_scale, w_out, ln1_g, ln1_b, w_up, w_down, ln2_g, ln2_b):
    depth = w_in.shape[0]
    assert depth == 1 and lb_logits.shape[0] == depth + 1
    alpha = (2.0 * depth) ** 0.25
    B, S, D = x.shape
    x2d = x.reshape(B * S, D)
    for l in range(depth):
        x2d = _layer(x2d, S, w_in[l], lb_logits, hgrn_norm_g[l], w_a[l], w_pool[l], pool_scale[l], w_out[l],
                     ln1_g[l], ln1_b[l], w_up[l], w_down[l], ln2_g[l], ln2_b[l], alpha)
    return x2d.reshape(B, S, D)
```

```python
import functools

import jax
import jax.numpy as jnp
from jax import lax
from jax.experimental import pallas as pl
from jax.experimental.pallas import tpu as pltpu

D_MODEL = 1024
HEADS = 8
HEAD_DIM = D_MODEL // HEADS
POOL_WINDOWS = (2, 4, 8, 16)
POOL_GROUP_DIM = D_MODEL // len(POOL_WINDOWS)
POOL_HISTORY = 16
D_FF = 4 * D_MODEL
IN_COLS = 7 * D_MODEL
LN_EPS = 1e-5
RMS_EPS = 1e-6

REC_BLOCK = 128
DIAG_CHUNK = 32
TILE_TOKENS = 256
FILL_COLS = 512
FF_CHUNK = 1024
LOAD_ROWS = 256
LOAD_SLOTS = 4
VMEM_LIMIT_BYTES = 60 * 1024 * 1024

BF16 = jnp.bfloat16
F32 = jnp.float32


def _sigmoid(z):
    return 0.5 * jnp.tanh(0.5 * z) + 0.5


def _dot(a, b):
    return jnp.dot(a, b, preferred_element_type=F32)


def _dot_tn(a, b):
    return lax.dot_general(a, b, (((0,), (0,)), ((), ())), preferred_element_type=F32)


def _layer_norm(y, g, b):
    mu = jnp.mean(y, axis=-1, keepdims=True)
    c = y - mu
    var = jnp.mean(c * c, axis=-1, keepdims=True)
    return c * lax.rsqrt(var + LN_EPS) * g + b


def _block_cumsum(tri2, z):
    z1 = z.astype(BF16)
    z2 = (z - z1.astype(F32)).astype(BF16)
    return _dot(tri2, jnp.concatenate([z1, z2], axis=0))


def _row_bcast(g, period, row):
    n = g.shape[0] // period
    g3 = g.reshape(n, period, g.shape[1])
    r = jnp.broadcast_to(g3[:, row:row + 1, :], g3.shape)
    return r.reshape(g.shape)


def _load_weight(src_chunk, dst, stage, sem):
    n_blocks, n_rows, cols = dst.shape
    per_block = n_rows // LOAD_ROWS
    n = n_blocks * per_block
    ahead = LOAD_SLOTS - 1
    assert n_rows % LOAD_ROWS == 0 and n >= ahead

    def split(i):
        return i // per_block, pl.multiple_of((i % per_block) * LOAD_ROWS, LOAD_ROWS)

    def copy(i):
        blk, r0 = split(i)
        slot = i % LOAD_SLOTS
        return pltpu.make_async_copy(src_chunk(blk, r0), stage.at[slot, :, pl.ds(0, cols)], sem.at[slot])

    for i in range(ahead):
        copy(i).start()

    def body(i, carry):
        @pl.when(i + ahead < n)
        def _():
            copy(i + ahead).start()

        copy(i).wait()
        blk, r0 = split(i)
        dst[blk, pl.ds(r0, LOAD_ROWS), :] = stage[i % LOAD_SLOTS, :, pl.ds(0, cols)].astype(BF16)
        return carry

    lax.fori_loop(0, n, body, 0)


def _block_kernel(alpha, seq_tiles, x_ref, w_in_hbm, lbl_ref, g_ref, wa_hbm, wp_hbm, ps_ref, wo_hbm,
                  l1g_ref, l1b_ref, wu_hbm, wd_hbm, l2g_ref, l2b_ref, out_ref,
                  state_ref, carry_ref, o_ref, y1_ref,
                  w_in_ref, wa_ref, wp_ref, wo_ref, wu_ref, wd_ref, stage_ref, load_sem):
    tm = x_ref.shape[0]
    t = pl.program_id(0)
    j = t % seq_tiles

    @pl.when(t == 0)
    def _():
        y1_ref[...] = jnp.zeros_like(y1_ref)
        rows = lambda r0: pl.ds(r0, LOAD_ROWS)
        col_block = lambda hbm: (lambda b, r0: hbm.at[rows(r0), pl.ds(pl.multiple_of(b * D_MODEL, D_MODEL), D_MODEL)])
        row_block = lambda hbm: (lambda b, r0: hbm.at[rows(pl.multiple_of(b * D_MODEL, D_MODEL) + r0), :])
        _load_weight(col_block(w_in_hbm), w_in_ref, stage_ref, load_sem)
        _load_weight(row_block(wa_hbm), wa_ref, stage_ref, load_sem)
        _load_weight(row_block(wp_hbm), wp_ref, stage_ref, load_sem)
        _load_weight(row_block(wo_hbm), wo_ref, stage_ref, load_sem)
        _load_weight(col_block(wu_hbm), wu_ref, stage_ref, load_sem)
        _load_weight(row_block(wd_hbm), wd_ref, stage_ref, load_sem)

    @pl.when(j == 0)
    def _():
        state_ref[...] = jnp.zeros_like(state_ref)
        carry_ref[...] = jnp.zeros_like(carry_ref)

    x = x_ref[...]
    xb = x.astype(BF16)

    def proj(c):
        return _dot(xb, w_in_ref[c])

    l0 = lbl_ref[0:1, :]
    l1 = lbl_ref[1:2, :]
    lmax = jnp.maximum(l0, l1)
    e0 = jnp.exp(l0 - lmax)
    e1 = jnp.exp(l1 - lmax)
    lb = e0 / (e0 + e1)
    f_mid = 0.5 * (1.0 + lb)
    f_half = 0.5 * (1.0 - lb)

    f_swing = f_half * jnp.tanh(0.5 * proj(1))
    k = f_half - f_swing
    logf = jnp.log(f_mid + f_swing)
    q = proj(0)
    q = (q * (0.5 * HEAD_DIM ** -0.5)) * (jnp.tanh(0.5 * q) + 1.0)
    v = proj(2).astype(BF16)

    L = REC_BLOCK
    assert tm == 2 * L
    PW = 2 * HEAD_DIM
    ti = lax.broadcasted_iota(jnp.int32, (L, 2 * L), 0)
    si = lax.broadcasted_iota(jnp.int32, (L, 2 * L), 1)
    tri2 = ((si % L) <= ti).astype(BF16)
    tp = lax.broadcasted_iota(jnp.int32, (L, PW), 0)
    lp = lax.broadcasted_iota(jnp.int32, (L, PW), 1)
    sp = lp % L
    m_diag = ((tp // DIAG_CHUNK) == (sp // DIAG_CHUNK)) & (sp <= tp)
    m_32 = ((tp // 64) == (sp // 64)) & ((tp % 64) >= 32) & ((sp % 64) < 32)
    m_64 = (tp >= 64) & (sp < 64)
    head_a = lp < HEAD_DIM

    row_a = lax.broadcasted_iota(jnp.int32, (PW, L), 0) < HEAD_DIM

    def block_diag_t(pair):
        kt = pair.T.astype(BF16)
        z = jnp.zeros_like(kt)
        return jnp.concatenate([jnp.where(row_a, kt, z), jnp.where(row_a, z, kt)], axis=1)

    def block_diag(pair):
        z = jnp.zeros_like(pair)
        return jnp.concatenate([jnp.where(head_a, pair, z), jnp.where(head_a, z, pair)], axis=0)

    n_blk = tm // L
    row_slices = [slice(i * L, (i + 1) * L) for i in range(n_blk)]
    cums = [_block_cumsum(tri2, logf[rows]) for rows in row_slices]

    def prepare(blk):
        rows = row_slices[blk]
        qb, kb, G = q[rows], k[rows], cums[blk]
        g_last = G[L - 1:L, :]
        ops = dict(
            v=v[rows],
            q_dec=(qb * jnp.exp(G)).astype(BF16),
            k_end=(kb * jnp.exp(g_last - G)).astype(BF16),
            decay=jnp.exp(g_last))
        r_d = _row_bcast(G, DIAG_CHUNK, DIAG_CHUNK // 2 - 1)
        ops["q_d"] = (qb * jnp.exp(G - r_d)).astype(BF16)
        ops["k_d"] = kb * jnp.exp(r_d - G)
        e_32 = jnp.exp(-jnp.abs(G - _row_bcast(G, 64, 31)))
        ops["q_32"] = (qb * e_32).astype(BF16)
        ops["k_32"] = kb * e_32
        e_64 = jnp.exp(-jnp.abs(G - _row_bcast(G, 128, 63)))
        ops["q_64"] = (qb * e_64).astype(BF16)
        ops["k_64"] = kb * e_64
        return ops

    units = []
    proj_out = {}

    def proj_unit(c, n):
        def run():
            proj_out[(c, n)] = _dot(xb, w_in_ref[c, :, n * FILL_COLS:(n + 1) * FILL_COLS])
        return run

    x1 = _layer_norm(y1_ref[...], l1g_ref[...], l1b_ref[...])
    x1b = x1.astype(BF16)
    n_ff = D_FF // FF_CHUNK
    assert FF_CHUNK == D_MODEL
    h_parts, d_parts = {}, {}

    def up_unit(c, n):
        def run():
            hh = jnp.maximum(_dot(x1b, wu_ref[c, :, n * FILL_COLS:(n + 1) * FILL_COLS]), 0.0)
            h_parts[(c, n)] = (hh * hh).astype(BF16)
        return run

    def down_unit(c, m):
        def run():
            hc = jnp.concatenate([h_parts[(c, n)] for n in range(FF_CHUNK // FILL_COLS)], axis=-1)
            d_parts[(c, m)] = _dot(hc, wd_ref[c, :, m * FILL_COLS:(m + 1) * FILL_COLS])
        return run

    ups = [[up_unit(c, n) for n in range(FF_CHUNK // FILL_COLS)] for c in range(n_ff)]
    downs = [[down_unit(c, m) for m in range(D_MODEL // FILL_COLS)] for c in range(n_ff)]
    projs = [[proj_unit(c, n) for n in range(D_MODEL // FILL_COLS)] for c in (3, 4, 5, 6)]
    units += projs[0] + projs[1] + projs[2] + projs[3]
    units += ups[0] + ups[1] + downs[0] + ups[2] + downs[1] + ups[3] + downs[2] + downs[3]

    def fill(n=1):
        for _ in range(n):
            if units:
                units.pop(0)()

    def recur(blk, ops):
        rows = row_slices[blk]
        probs = []
        for p in range(HEADS // 2):
            cols = slice(p * PW, (p + 1) * PW)
            s_d = _dot(ops["q_d"][:, cols], block_diag_t(ops["k_d"][:, cols]))
            s_32 = _dot(ops["q_32"][:, cols], block_diag_t(ops["k_32"][:, cols]))
            s_64 = _dot(ops["q_64"][:, cols], block_diag_t(ops["k_64"][:, cols]))
            fill()
            probs.append(jnp.where(m_diag, s_d, jnp.where(m_32, s_32, jnp.where(m_64, s_64, 0.0))).astype(BF16))
        for p in range(HEADS // 2):
            cols = slice(p * PW, (p + 1) * PW)
            state = state_ref[p]
            v_bd = block_diag(ops["v"][:, cols])
            o_p = _dot(jnp.concatenate([probs[p], ops["q_dec"][:, cols]], axis=1),
                       jnp.concatenate([v_bd, block_diag(state.astype(BF16))], axis=0))
            k_pair = ops["k_end"][:, cols]
            k_rows = jnp.concatenate([k_pair[:, :HEAD_DIM], k_pair[:, HEAD_DIM:]], axis=0)
            kv = _dot_tn(k_rows, v_bd)
            dec = ops["decay"][:, cols]
            decay_col = jnp.concatenate(
                [jnp.broadcast_to(dec[:, h * HEAD_DIM:(h + 1) * HEAD_DIM], (HEAD_DIM, HEAD_DIM)).T for h in range(2)], axis=1)
            fill()
            state_ref[p] = decay_col * state + kv
            for half in range(2):
                o_h = o_p[:, half * HEAD_DIM:(half + 1) * HEAD_DIM]
                o_h = o_h * lax.rsqrt(jnp.mean(o_h * o_h, axis=-1, keepdims=True) + RMS_EPS)
                o_ref[rows, p * PW + half * HEAD_DIM:p * PW + (half + 1) * HEAD_DIM] = o_h

    def pool_branch():
        pv = jnp.concatenate([proj_out[(4, n)] for n in range(D_MODEL // FILL_COLS)], axis=-1)
        ext = jnp.concatenate([carry_ref[...], pv], axis=0)
        carry_ref[...] = pv[tm - POOL_HISTORY:, :]
        pos = j * tm + lax.broadcasted_iota(jnp.int32, (tm, 1), 0)
        b_parts = []
        for gi, w in enumerate(POOL_WINDOWS):
            cols = slice(gi * POOL_GROUP_DIM, (gi + 1) * POOL_GROUP_DIM)
            acc = ext[:, cols]
            span = 1
            while span < w:
                acc = acc + pltpu.roll(acc, span, 0)
                span *= 2
            inv_count = 1.0 / jnp.minimum(pos + 1, w).astype(F32)
            pooled = acc[POOL_HISTORY:, :] * inv_count - pv[:, cols]
            b_parts.append(_dot(pooled.astype(BF16), wp_ref[0, gi * POOL_GROUP_DIM:(gi + 1) * POOL_GROUP_DIM, :]))
        return jnp.concatenate(b_parts, axis=-1) * ps_ref[...]

    full = lambda c, rows: jnp.concatenate([proj_out[(c, n)][rows] for n in range(D_MODEL // FILL_COLS)], axis=-1)

    def tail(blk, b):
        rows = row_slices[blk]
        o = (o_ref[rows, :] * (0.5 * g_ref[...])) * (jnp.tanh(0.5 * full(3, rows)) + 1.0)
        a = _dot(o.astype(BF16), wa_ref[0])
        fill()
        merged = _sigmoid(full(5, rows)) * a + _sigmoid(full(6, rows)) * b[rows]
        mix = _dot(merged.astype(BF16), wo_ref[0])
        fill()
        y1_ref[rows, :] = alpha * x[rows] + mix

    ops = prepare(0)
    fill(8)
    recur(0, ops)
    b = pool_branch()
    ops = prepare(1)
    fill(2)
    tail(0, b)
    recur(1, ops)
    fill(len(units))

    mlp = jnp.concatenate([sum(d_parts[(c, m)] for c in range(n_ff)) for m in range(D_MODEL // FILL_COLS)], axis=-1)
    out_ref[...] = _layer_norm(alpha * x1 + mlp, l2g_ref[...], l2b_ref[...])
    tail(1, b)


def _const_spec(shape):
    nd = len(shape)
    return pl.BlockSpec(shape, lambda *_: (0,) * nd, pipeline_mode=pl.Buffered(1))


def _layer(x2d, seq_len, w_in, lb_logits, norm_g, w_a, w_pool, pool_scale, w_out, ln1_g, ln1_b,
           w_up, w_down, ln2_g, ln2_b, alpha):
    N, D = x2d.shape
    tm = TILE_TOKENS
    n_tiles = N // tm
    row = lambda p: p.reshape(1, D).astype(F32)
    hbm = pl.BlockSpec(memory_space=pl.ANY)
    return pl.pallas_call(
        functools.partial(_block_kernel, alpha, seq_len // tm),
        grid=(n_tiles + 1,),
        in_specs=[
            pl.BlockSpec((tm, D), lambda t: (jnp.minimum(t, n_tiles - 1), 0)),
            hbm,
            _const_spec(lb_logits.shape),
            _const_spec((1, D)),
            hbm,
            hbm,
            _const_spec((1, D)),
            hbm,
            _const_spec((1, D)),
            _const_spec((1, D)),
            hbm,
            hbm,
            _const_spec((1, D)),
            _const_spec((1, D)),
        ],
        out_specs=pl.BlockSpec((tm, D), lambda t: (jnp.maximum(t - 1, 0), 0)),
        out_shape=jax.ShapeDtypeStruct((N, D), F32),
        scratch_shapes=[
            pltpu.VMEM((HEADS // 2, HEAD_DIM, 2 * HEAD_DIM), F32),
            pltpu.VMEM((POOL_HISTORY, D), F32),
            pltpu.VMEM((tm, D), F32),
            pltpu.VMEM((tm, D), F32),
            pltpu.VMEM((IN_COLS // D, D, D), BF16),
            pltpu.VMEM((1, D, D), BF16),
            pltpu.VMEM((1, D, POOL_GROUP_DIM), BF16),
            pltpu.VMEM((1, D, D), BF16),
            pltpu.VMEM((D_FF // D, D, D), BF16),
            pltpu.VMEM((D_FF // D, D, D), BF16),
            pltpu.VMEM((LOAD_SLOTS, LOAD_ROWS, D), F32),
            pltpu.SemaphoreType.DMA((LOAD_SLOTS,)),
        ],
        compiler_params=pltpu.CompilerParams(
            dimension_semantics=("arbitrary",),
            vmem_limit_bytes=VMEM_LIMIT_BYTES),
        name="hgrn2_pool_mlp_block",
    )(x2d, w_in, lb_logits.astype(F32), row(norm_g), w_a, w_pool.reshape(D, POOL_GROUP_DIM),
      row(pool_scale), w_out, row(ln1_g), row(ln1_b), w_up, w_down, row(ln2_g), row(ln2_b))


def kernel(x, w_in, lb_logits, hgrn_norm_g, w_a, w_pool, pool_scale, w_out, ln1_g, ln1_b, w_up, w_down, ln2_g, ln2_b):
    depth = w_in.shape[0]
    assert depth == 1 and lb_logits.shape[0] == depth + 1
    alpha = (2.0 * depth) ** 0.25
    B, S, D = x.shape
    x2d = x.reshape(B * S, D)
    for l in range(depth):
        x2d = _layer(x2d, S, w_in[l], lb_logits, hgrn_norm_g[l], w_a[l], w_pool[l], pool_scale[l], w_out[l],
                     ln1_g[l], ln1_b[l], w_up[l], w_down[l], ln2_g[l], ln2_b[l], alpha)
    return x2d.reshape(B, S, D)
```

```python
import functools

import jax
import jax.numpy as jnp
from jax import lax
from jax.experimental import pallas as pl
from jax.experimental.pallas import tpu as pltpu

D_MODEL = 1024
HEADS = 8
HEAD_DIM = D_MODEL // HEADS
POOL_WINDOWS = (2, 4, 8, 16)
POOL_GROUP_DIM = D_MODEL // len(POOL_WINDOWS)
POOL_HISTORY = 16
D_FF = 4 * D_MODEL
IN_COLS = 7 * D_MODEL
LN_EPS = 1e-5
RMS_EPS = 1e-6

REC_BLOCK = 128
DIAG_CHUNK = 32
TILE_TOKENS = 256
FILL_COLS = 512
FF_CHUNK = 1024
LOAD_ROWS = 256
LOAD_SLOTS = 5
VMEM_LIMIT_BYTES = 60 * 1024 * 1024

BF16 = jnp.bfloat16
F32 = jnp.float32


def _sigmoid(z):
    return 0.5 * jnp.tanh(0.5 * z) + 0.5


def _dot(a, b):
    return jnp.dot(a, b, preferred_element_type=F32)


def _dot_nt(a, b):
    return lax.dot_general(a, b, (((1,), (1,)), ((), ())), preferred_element_type=F32)


def _dot_tn(a, b):
    return lax.dot_general(a, b, (((0,), (0,)), ((), ())), preferred_element_type=F32)


def _layer_norm(y, g, b):
    mu = jnp.mean(y, axis=-1, keepdims=True)
    c = y - mu
    var = jnp.mean(c * c, axis=-1, keepdims=True)
    return c * lax.rsqrt(var + LN_EPS) * g + b


def _block_cumsum(tri2, z):
    z1 = z.astype(BF16)
    z2 = (z - z1.astype(F32)).astype(BF16)
    return _dot(tri2, jnp.concatenate([z1, z2], axis=0))


def _row_bcast(g, period, row):
    n = g.shape[0] // period
    g3 = g.reshape(n, period, g.shape[1])
    r = jnp.broadcast_to(g3[:, row:row + 1, :], g3.shape)
    return r.reshape(g.shape)


def _load_weights(weights, stage, sem):
    ahead = LOAD_SLOTS - 1
    counts = []
    for _, dst in weights:
        n_blocks, n_rows, _ = dst.shape
        assert n_rows % LOAD_ROWS == 0
        counts.append(n_blocks * (n_rows // LOAD_ROWS))
    assert min(counts) >= ahead
    starts = [sum(counts[:w]) for w in range(len(weights))]

    def copy(w, i):
        src_chunk, dst = weights[w]
        per_block = dst.shape[1] // LOAD_ROWS
        r0 = pl.multiple_of((i % per_block) * LOAD_ROWS, LOAD_ROWS)
        slot = (starts[w] + i) % LOAD_SLOTS
        return pltpu.make_async_copy(src_chunk(i // per_block, r0), stage.at[slot, :, pl.ds(0, dst.shape[2])], sem.at[slot])

    for i in range(ahead):
        copy(0, i).start()

    for w, (_, dst) in enumerate(weights):
        n = counts[w]
        per_block = dst.shape[1] // LOAD_ROWS
        cols = dst.shape[2]

        def body(i, carry, w=w, n=n, per_block=per_block, cols=cols, dst=dst):
            @pl.when(i + ahead < n)
            def _():
                copy(w, i + ahead).start()

            if w + 1 < len(weights):
                @pl.when(i + ahead >= n)
                def _():
                    copy(w + 1, i + ahead - n).start()

            copy(w, i).wait()
            r0 = pl.multiple_of((i % per_block) * LOAD_ROWS, LOAD_ROWS)
            slot = (starts[w] + i) % LOAD_SLOTS
            dst[i // per_block, pl.ds(r0, LOAD_ROWS), :] = stage[slot, :, pl.ds(0, cols)].astype(BF16)
            return carry

        lax.fori_loop(0, n, body, 0)


def _block_kernel(alpha, seq_tiles, x_ref, w_in_hbm, lbl_ref, g_ref, wa_hbm, wp_hbm, ps_ref, wo_hbm,
                  l1g_ref, l1b_ref, wu_hbm, wd_hbm, l2g_ref, l2b_ref, out_ref,
                  state_ref, carry_ref, o_ref, y1_ref,
                  w_in_ref, wa_ref, wp_ref, wo_ref, wu_ref, wd_ref, stage_ref, load_sem):
    tm = x_ref.shape[0]
    t = pl.program_id(0)
    j = t % seq_tiles

    @pl.when(t == 0)
    def _():
        y1_ref[...] = jnp.zeros_like(y1_ref)
        rows = lambda r0: pl.ds(r0, LOAD_ROWS)
        col_block = lambda hbm: (lambda b, r0: hbm.at[rows(r0), pl.ds(pl.multiple_of(b * D_MODEL, D_MODEL), D_MODEL)])
        row_block = lambda hbm: (lambda b, r0: hbm.at[rows(pl.multiple_of(b * D_MODEL, D_MODEL) + r0), :])
        _load_weights([(col_block(w_in_hbm), w_in_ref), (row_block(wa_hbm), wa_ref), (row_block(wp_hbm), wp_ref),
                       (row_block(wo_hbm), wo_ref), (col_block(wu_hbm), wu_ref), (row_block(wd_hbm), wd_ref)],
                      stage_ref, load_sem)

    @pl.when(j == 0)
    def _():
        state_ref[...] = jnp.zeros_like(state_ref)
        carry_ref[...] = jnp.zeros_like(carry_ref)

    x = x_ref[...]
    xb = x.astype(BF16)

    def proj(c):
        return _dot(xb, w_in_ref[c])

    l0 = lbl_ref[0:1, :]
    l1 = lbl_ref[1:2, :]
    lmax = jnp.maximum(l0, l1)
    e0 = jnp.exp(l0 - lmax)
    e1 = jnp.exp(l1 - lmax)
    lb = e0 / (e0 + e1)
    f_mid = 0.5 * (1.0 + lb)
    f_half = 0.5 * (1.0 - lb)

    f_swing = f_half * jnp.tanh(0.5 * proj(1))
    k = f_half - f_swing
    logf = jnp.log(f_mid + f_swing)
    q = proj(0)
    q = (q * (0.5 * HEAD_DIM ** -0.5)) * (jnp.tanh(0.5 * q) + 1.0)
    v = proj(2).astype(BF16)

    L = REC_BLOCK
    assert tm == 2 * L
    PW = 2 * HEAD_DIM
    ti = lax.broadcasted_iota(jnp.int32, (L, 2 * L), 0)
    si = lax.broadcasted_iota(jnp.int32, (L, 2 * L), 1)
    tri2 = ((si % L) <= ti).astype(BF16)
    tp = lax.broadcasted_iota(jnp.int32, (L, PW), 0)
    lp = lax.broadcasted_iota(jnp.int32, (L, PW), 1)
    sp = lp % L
    m_diag = ((tp // DIAG_CHUNK) == (sp // DIAG_CHUNK)) & (sp <= tp)
    m_32 = ((tp // 64) == (sp // 64)) & ((tp % 64) >= 32) & ((sp % 64) < 32)
    m_64 = (tp >= 64) & (sp < 64)
    head_a = lp < HEAD_DIM

    def block_diag(pair):
        z = jnp.zeros_like(pair)
        return jnp.concatenate([jnp.where(head_a, pair, z), jnp.where(head_a, z, pair)], axis=0)

    n_blk = tm // L
    row_slices = [slice(i * L, (i + 1) * L) for i in range(n_blk)]
    cums = [_block_cumsum(tri2, logf[rows]) for rows in row_slices]

    def prepare(blk):
        rows = row_slices[blk]
        qb, kb, G = q[rows], k[rows], cums[blk]
        g_last = G[L - 1:L, :]
        ops = dict(
            v=v[rows],
            q_dec=(qb * jnp.exp(G)).astype(BF16),
            k_end=(kb * jnp.exp(g_last - G)).astype(BF16),
            decay=jnp.exp(g_last))
        r_d = _row_bcast(G, DIAG_CHUNK, DIAG_CHUNK // 2 - 1)
        ops["q_d"] = (qb * jnp.exp(G - r_d)).astype(BF16)
        ops["k_d"] = (kb * jnp.exp(r_d - G)).astype(BF16)
        e_32 = jnp.exp(-jnp.abs(G - _row_bcast(G, 64, 31)))
        ops["q_32"] = (qb * e_32).astype(BF16)
        ops["k_32"] = (kb * e_32).astype(BF16)
        e_64 = jnp.exp(-jnp.abs(G - _row_bcast(G, 128, 63)))
        ops["q_64"] = (qb * e_64).astype(BF16)
        ops["k_64"] = (kb * e_64).astype(BF16)
        return ops

    units = []
    proj_out = {}

    def proj_unit(c, n):
        def run():
            proj_out[(c, n)] = _dot(xb, w_in_ref[c, :, n * FILL_COLS:(n + 1) * FILL_COLS])
        return run

    x1 = _layer_norm(y1_ref[...], l1g_ref[...], l1b_ref[...])
    x1b = x1.astype(BF16)
    n_ff = D_FF // FF_CHUNK
    assert FF_CHUNK == D_MODEL
    h_parts, d_parts = {}, {}

    def up_unit(c, n):
        def run():
            hh = jnp.maximum(_dot(x1b, wu_ref[c, :, n * FILL_COLS:(n + 1) * FILL_COLS]), 0.0)
            h_parts[(c, n)] = (hh * hh).astype(BF16)
        return run

    def down_unit(c, m):
        def run():
            hc = jnp.concatenate([h_parts[(c, n)] for n in range(FF_CHUNK // FILL_COLS)], axis=-1)
            d_parts[(c, m)] = _dot(hc, wd_ref[c, :, m * FILL_COLS:(m + 1) * FILL_COLS])
        return run

    ups = [[up_unit(c, n) for n in range(FF_CHUNK // FILL_COLS)] for c in range(n_ff)]
    downs = [[down_unit(c, m) for m in range(D_MODEL // FILL_COLS)] for c in range(n_ff)]
    projs = [[proj_unit(c, n) for n in range(D_MODEL // FILL_COLS)] for c in (3, 4, 5, 6)]
    units += projs[0] + projs[1] + projs[2] + projs[3]
    units += ups[0] + ups[1] + downs[0] + ups[2] + downs[1] + ups[3] + downs[2] + downs[3]

    def fill(n=1):
        for _ in range(n):
            if units:
                units.pop(0)()

    def recur(blk, ops):
        rows = row_slices[blk]
        probs = []
        for p in range(HEADS // 2):
            cols = slice(p * PW, (p + 1) * PW)
            s_d = _dot_nt(ops["q_d"][:, cols], block_diag(ops["k_d"][:, cols]))
            s_32 = _dot_nt(ops["q_32"][:, cols], block_diag(ops["k_32"][:, cols]))
            s_64 = _dot_nt(ops["q_64"][:, cols], block_diag(ops["k_64"][:, cols]))
            fill()
            probs.append(jnp.where(m_diag, s_d, jnp.where(m_32, s_32, jnp.where(m_64, s_64, 0.0))).astype(BF16))
        for p in range(HEADS // 2):
            cols = slice(p * PW, (p + 1) * PW)
            state = state_ref[p]
            v_bd = block_diag(ops["v"][:, cols])
            o_p = _dot(probs[p], v_bd) + _dot(ops["q_dec"][:, cols], block_diag(state.astype(BF16)))
            k_pair = ops["k_end"][:, cols]
            k_rows = jnp.concatenate([k_pair[:, :HEAD_DIM], k_pair[:, HEAD_DIM:]], axis=0)
            kv = _dot_tn(k_rows, v_bd)
            dec = ops["decay"][:, cols]
            decay_col = jnp.concatenate(
                [jnp.broadcast_to(dec[:, h * HEAD_DIM:(h + 1) * HEAD_DIM], (HEAD_DIM, HEAD_DIM)).T for h in range(2)], axis=1)
            fill()
            state_ref[p] = decay_col * state + kv
            for half in range(2):
                o_h = o_p[:, half * HEAD_DIM:(half + 1) * HEAD_DIM]
                o_h = o_h * lax.rsqrt(jnp.mean(o_h * o_h, axis=-1, keepdims=True) + RMS_EPS)
                o_ref[rows, p * PW + half * HEAD_DIM:p * PW + (half + 1) * HEAD_DIM] = o_h

    def pool_branch():
        pv = jnp.concatenate([proj_out[(4, n)] for n in range(D_MODEL // FILL_COLS)], axis=-1)
        ext = jnp.concatenate([carry_ref[...], pv], axis=0)
        carry_ref[...] = pv[tm - POOL_HISTORY:, :]
        pos = j * tm + lax.broadcasted_iota(jnp.int32, (tm, 1), 0)
        b_parts = []
        for gi, w in enumerate(POOL_WINDOWS):
            cols = slice(gi * POOL_GROUP_DIM, (gi + 1) * POOL_GROUP_DIM)
            acc = ext[:, cols]
            span = 1
            while span < w:
                acc = acc + pltpu.roll(acc, span, 0)
                span *= 2
            inv_count = 1.0 / jnp.minimum(pos + 1, w).astype(F32)
            pooled = acc[POOL_HISTORY:, :] * inv_count - pv[:, cols]
            b_parts.append(_dot(pooled.astype(BF16), wp_ref[0, gi * POOL_GROUP_DIM:(gi + 1) * POOL_GROUP_DIM, :]))
        return jnp.concatenate(b_parts, axis=-1) * ps_ref[...]

    full = lambda c, rows: jnp.concatenate([proj_out[(c, n)][rows] for n in range(D_MODEL // FILL_COLS)], axis=-1)

    def tail(blk, b):
        rows = row_slices[blk]
        o = (o_ref[rows, :] * (0.5 * g_ref[...])) * (jnp.tanh(0.5 * full(3, rows)) + 1.0)
        a = _dot(o.astype(BF16), wa_ref[0])
        fill()
        merged = _sigmoid(full(5, rows)) * a + _sigmoid(full(6, rows)) * b[rows]
        mix = _dot(merged.astype(BF16), wo_ref[0])
        fill()
        y1_ref[rows, :] = alpha * x[rows] + mix

    ops = prepare(0)
    fill(6)
    recur(0, ops)
    b = pool_branch()
    ops = prepare(1)
    fill(2)
    tail(0, b)
    recur(1, ops)
    fill(len(units))

    mlp = jnp.concatenate([sum(d_parts[(c, m)] for c in range(n_ff)) for m in range(D_MODEL // FILL_COLS)], axis=-1)
    out_ref[...] = _layer_norm(alpha * x1 + mlp, l2g_ref[...], l2b_ref[...])
    tail(1, b)


def _const_spec(shape):
    nd = len(shape)
    return pl.BlockSpec(shape, lambda *_: (0,) * nd, pipeline_mode=pl.Buffered(1))


def _layer(x2d, seq_len, w_in, lb_logits, norm_g, w_a, w_pool, pool_scale, w_out, ln1_g, ln1_b,
           w_up, w_down, ln2_g, ln2_b, alpha):
    N, D = x2d.shape
    tm = TILE_TOKENS
    n_tiles = N // tm
    row = lambda p: p.reshape(1, D).astype(F32)
    hbm = pl.BlockSpec(memory_space=pl.ANY)
    return pl.pallas_call(
        functools.partial(_block_kernel, alpha, seq_len // tm),
        grid=(n_tiles + 1,),
        in_specs=[
            pl.BlockSpec((tm, D), lambda t: (jnp.minimum(t, n_tiles - 1), 0)),
            hbm,
            _const_spec(lb_logits.shape),
            _const_spec((1, D)),
            hbm,
            hbm,
            _const_spec((1, D)),
            hbm,
            _const_spec((1, D)),
            _const_spec((1, D)),
            hbm,
            hbm,
            _const_spec((1, D)),
            _const_spec((1, D)),
        ],
        out_specs=pl.BlockSpec((tm, D), lambda t: (jnp.maximum(t - 1, 0), 0)),
        out_shape=jax.ShapeDtypeStruct((N, D), F32),
        scratch_shapes=[
            pltpu.VMEM((HEADS // 2, HEAD_DIM, 2 * HEAD_DIM), F32),
            pltpu.VMEM((POOL_HISTORY, D), F32),
            pltpu.VMEM((tm, D), F32),
            pltpu.VMEM((tm, D), F32),
            pltpu.VMEM((IN_COLS // D, D, D), BF16),
            pltpu.VMEM((1, D, D), BF16),
            pltpu.VMEM((1, D, POOL_GROUP_DIM), BF16),
            pltpu.VMEM((1, D, D), BF16),
            pltpu.VMEM((D_FF // D, D, D), BF16),
            pltpu.VMEM((D_FF // D, D, D), BF16),
            pltpu.VMEM((LOAD_SLOTS, LOAD_ROWS, D), F32),
            pltpu.SemaphoreType.DMA((LOAD_SLOTS,)),
        ],
        compiler_params=pltpu.CompilerParams(
            dimension_semantics=("arbitrary",),
            vmem_limit_bytes=VMEM_LIMIT_BYTES),
        name="hgrn2_pool_mlp_block",
    )(x2d, w_in, lb_logits.astype(F32), row(norm_g), w_a, w_pool.reshape(D, POOL_GROUP_DIM),
      row(pool_scale), w_out, row(ln1_g), row(ln1_b), w_up, w_down, row(ln2_g), row(ln2_b))


def kernel(x, w_in, lb_logits, hgrn_norm_g, w_a, w_pool, pool_scale, w_out, ln1_g, ln1_b, w_up, w_down, ln2_g, ln2_b):
    depth = w_in.shape[0]
    assert depth == 1 and lb_logits.shape[0] == depth + 1
    alpha = (2.0 * depth) ** 0.25
    B, S, D = x.shape
    x2d = x.reshape(B * S, D)
    for l in range(depth):
        x2d = _layer(x2d, S, w_in[l], lb_logits, hgrn_norm_g[l], w_a[l], w_pool[l], pool_scale[l], w_out[l],
                     ln1_g[l], ln1_b[l], w_up[l], w_down[l], ln2_g[l], ln2_b[l], alpha)
    return x2d.reshape(B, S, D)
```

```python
import functools

import jax
import jax.numpy as jnp
from jax import lax
from jax.experimental import pallas as pl
from jax.experimental.pallas import tpu as pltpu

D_MODEL = 1024
HEADS = 8
HEAD_DIM = D_MODEL // HEADS
POOL_WINDOWS = (2, 4, 8, 16)
POOL_GROUP_DIM = D_MODEL // len(POOL_WINDOWS)
POOL_HISTORY = 16
D_FF = 4 * D_MODEL
IN_COLS = 7 * D_MODEL
LN_EPS = 1e-5
RMS_EPS = 1e-6

REC_BLOCK = 128
DIAG_CHUNK = 32
TILE_TOKENS = 256
FILL_COLS = 512
FF_CHUNK = 1024
LOAD_ROWS = 256
LOAD_SLOTS = 5
VMEM_LIMIT_BYTES = 60 * 1024 * 1024

BF16 = jnp.bfloat16
F32 = jnp.float32


def _sigmoid(z):
    return 0.5 * jnp.tanh(0.5 * z) + 0.5


def _dot(a, b):
    return jnp.dot(a, b, preferred_element_type=F32)


def _dot_tn(a, b):
    return lax.dot_general(a, b, (((0,), (0,)), ((), ())), preferred_element_type=F32)


def _layer_norm(y, g, b):
    mu = jnp.mean(y, axis=-1, keepdims=True)
    c = y - mu
    var = jnp.mean(c * c, axis=-1, keepdims=True)
    return c * lax.rsqrt(var + LN_EPS) * g + b


def _block_cumsum(tri2, z):
    z1 = z.astype(BF16)
    z2 = (z - z1.astype(F32)).astype(BF16)
    return _dot(tri2, jnp.concatenate([z1, z2], axis=0))


def _row_bcast(g, period, row):
    n = g.shape[0] // period
    g3 = g.reshape(n, period, g.shape[1])
    r = jnp.broadcast_to(g3[:, row:row + 1, :], g3.shape)
    return r.reshape(g.shape)


def _load_weights(weights, stage, sem):
    ahead = LOAD_SLOTS - 1
    counts = []
    for _, dst in weights:
        n_blocks, n_rows, _ = dst.shape
        assert n_rows % LOAD_ROWS == 0
        counts.append(n_blocks * (n_rows // LOAD_ROWS))
    assert min(counts) >= ahead
    starts = [sum(counts[:w]) for w in range(len(weights))]

    def copy(w, i):
        src_chunk, dst = weights[w]
        per_block = dst.shape[1] // LOAD_ROWS
        r0 = pl.multiple_of((i % per_block) * LOAD_ROWS, LOAD_ROWS)
        slot = (starts[w] + i) % LOAD_SLOTS
        return pltpu.make_async_copy(src_chunk(i // per_block, r0), stage.at[slot, :, pl.ds(0, dst.shape[2])], sem.at[slot])

    for i in range(ahead):
        copy(0, i).start()

    for w, (_, dst) in enumerate(weights):
        n = counts[w]
        per_block = dst.shape[1] // LOAD_ROWS
        cols = dst.shape[2]

        def body(i, carry, w=w, n=n, per_block=per_block, cols=cols, dst=dst):
            @pl.when(i + ahead < n)
            def _():
                copy(w, i + ahead).start()

            if w + 1 < len(weights):
                @pl.when(i + ahead >= n)
                def _():
                    copy(w + 1, i + ahead - n).start()

            copy(w, i).wait()
            r0 = pl.multiple_of((i % per_block) * LOAD_ROWS, LOAD_ROWS)
            slot = (starts[w] + i) % LOAD_SLOTS
            dst[i // per_block, pl.ds(r0, LOAD_ROWS), :] = stage[slot, :, pl.ds(0, cols)].astype(BF16)
            return carry

        lax.fori_loop(0, n, body, 0)


def _block_kernel(alpha, seq_tiles, x_ref, w_in_hbm, lbl_ref, g_ref, wa_hbm, wp_hbm, ps_ref, wo_hbm,
                  l1g_ref, l1b_ref, wu_hbm, wd_hbm, l2g_ref, l2b_ref, out_ref,
                  state_ref, carry_ref, o_ref, y1_ref,
                  w_in_ref, wa_ref, wp_ref, wo_ref, wu_ref, wd_ref, stage_ref, load_sem):
    tm = x_ref.shape[0]
    t = pl.program_id(0)
    j = t % seq_tiles

    @pl.when(t == 0)
    def _():
        y1_ref[...] = jnp.zeros_like(y1_ref)
        rows = lambda r0: pl.ds(r0, LOAD_ROWS)
        col_block = lambda hbm: (lambda b, r0: hbm.at[rows(r0), pl.ds(pl.multiple_of(b * D_MODEL, D_MODEL), D_MODEL)])
        row_block = lambda hbm: (lambda b, r0: hbm.at[rows(pl.multiple_of(b * D_MODEL, D_MODEL) + r0), :])
        _load_weights([(col_block(w_in_hbm), w_in_ref), (row_block(wa_hbm), wa_ref), (row_block(wp_hbm), wp_ref),
                       (row_block(wo_hbm), wo_ref), (col_block(wu_hbm), wu_ref), (row_block(wd_hbm), wd_ref)],
                      stage_ref, load_sem)

    @pl.when(j == 0)
    def _():
        state_ref[...] = jnp.zeros_like(state_ref)
        carry_ref[...] = jnp.zeros_like(carry_ref)

    x = x_ref[...]
    xb = x.astype(BF16)

    def proj(c):
        return _dot(xb, w_in_ref[c])

    l0 = lbl_ref[0:1, :]
    l1 = lbl_ref[1:2, :]
    lmax = jnp.maximum(l0, l1)
    e0 = jnp.exp(l0 - lmax)
    e1 = jnp.exp(l1 - lmax)
    lb = e0 / (e0 + e1)
    f_mid = 0.5 * (1.0 + lb)
    f_half = 0.5 * (1.0 - lb)

    f_swing = f_half * jnp.tanh(0.5 * proj(1))
    k = f_half - f_swing
    logf = jnp.log(f_mid + f_swing)
    q = proj(0)
    q = (q * (0.5 * HEAD_DIM ** -0.5)) * (jnp.tanh(0.5 * q) + 1.0)
    v = proj(2).astype(BF16)

    L = REC_BLOCK
    assert tm == 2 * L
    PW = 2 * HEAD_DIM
    ti = lax.broadcasted_iota(jnp.int32, (L, 2 * L), 0)
    si = lax.broadcasted_iota(jnp.int32, (L, 2 * L), 1)
    tri2 = ((si % L) <= ti).astype(BF16)
    tp = lax.broadcasted_iota(jnp.int32, (L, PW), 0)
    lp = lax.broadcasted_iota(jnp.int32, (L, PW), 1)
    sp = lp % L
    m_diag = ((tp // DIAG_CHUNK) == (sp // DIAG_CHUNK)) & (sp <= tp)
    m_32 = ((tp // 64) == (sp // 64)) & ((tp % 64) >= 32) & ((sp % 64) < 32)
    m_64 = (tp >= 64) & (sp < 64)
    head_a = lp < HEAD_DIM

    row_a = lax.broadcasted_iota(jnp.int32, (PW, L), 0) < HEAD_DIM

    def block_diag_t(pair):
        kt = pair.T.astype(BF16)
        z = jnp.zeros_like(kt)
        return jnp.concatenate([jnp.where(row_a, kt, z), jnp.where(row_a, z, kt)], axis=1)

    def block_diag(pair):
        z = jnp.zeros_like(pair)
        return jnp.concatenate([jnp.where(head_a, pair, z), jnp.where(head_a, z, pair)], axis=0)

    n_blk = tm // L
    row_slices = [slice(i * L, (i + 1) * L) for i in range(n_blk)]
    cums = [_block_cumsum(tri2, logf[rows]) for rows in row_slices]

    def prepare(blk):
        rows = row_slices[blk]
        qb, kb, G = q[rows], k[rows], cums[blk]
        g_last = G[L - 1:L, :]
        ops = dict(
            v=v[rows],
            q_dec=(qb * jnp.exp(G)).astype(BF16),
            k_end=(kb * jnp.exp(g_last - G)).astype(BF16),
            decay=jnp.exp(g_last))
        r_d = _row_bcast(G, DIAG_CHUNK, DIAG_CHUNK // 2 - 1)
        ops["q_d"] = (qb * jnp.exp(G - r_d)).astype(BF16)
        ops["k_d"] = kb * jnp.exp(r_d - G)
        e_32 = jnp.exp(-jnp.abs(G - _row_bcast(G, 64, 31)))
        ops["q_32"] = (qb * e_32).astype(BF16)
        ops["k_32"] = kb * e_32
        e_64 = jnp.exp(-jnp.abs(G - _row_bcast(G, 128, 63)))
        ops["q_64"] = (qb * e_64).astype(BF16)
        ops["k_64"] = kb * e_64
        return ops

    units = []
    proj_out = {}

    def proj_unit(c, n):
        def run():
            proj_out[(c, n)] = _dot(xb, w_in_ref[c, :, n * FILL_COLS:(n + 1) * FILL_COLS])
        return run

    x1 = _layer_norm(y1_ref[...], l1g_ref[...], l1b_ref[...])
    x1b = x1.astype(BF16)
    n_ff = D_FF // FF_CHUNK
    assert FF_CHUNK == D_MODEL
    h_parts, d_parts = {}, {}

    def up_unit(c, n):
        def run():
            hh = jnp.maximum(_dot(x1b, wu_ref[c, :, n * FILL_COLS:(n + 1) * FILL_COLS]), 0.0)
            h_parts[(c, n)] = (hh * hh).astype(BF16)
        return run

    def down_unit(c, m):
        def run():
            hc = jnp.concatenate([h_parts[(c, n)] for n in range(FF_CHUNK // FILL_COLS)], axis=-1)
            d_parts[(c, m)] = _dot(hc, wd_ref[c, :, m * FILL_COLS:(m + 1) * FILL_COLS])
        return run

    ups = [[up_unit(c, n) for n in range(FF_CHUNK // FILL_COLS)] for c in range(n_ff)]
    downs = [[down_unit(c, m) for m in range(D_MODEL // FILL_COLS)] for c in range(n_ff)]
    projs = [[proj_unit(c, n) for n in range(D_MODEL // FILL_COLS)] for c in (3, 4, 5, 6)]
    units += projs[0] + projs[1] + projs[2] + projs[3]
    units += ups[0] + ups[1] + downs[0] + ups[2] + downs[1] + ups[3] + downs[2] + downs[3]

    def fill(n=1):
        for _ in range(n):
            if units:
                units.pop(0)()

    def recur(blk, ops):
        rows = row_slices[blk]
        probs = []
        for p in range(HEADS // 2):
            cols = slice(p * PW, (p + 1) * PW)
            s_d = _dot(ops["q_d"][:, cols], block_diag_t(ops["k_d"][:, cols]))
            s_32 = _dot(ops["q_32"][:, cols], block_diag_t(ops["k_32"][:, cols]))
            s_64 = _dot(ops["q_64"][:, cols], block_diag_t(ops["k_64"][:, cols]))
            fill()
            probs.append(jnp.where(m_diag, s_d, jnp.where(m_32, s_32, jnp.where(m_64, s_64, 0.0))).astype(BF16))
        for p in range(HEADS // 2):
            cols = slice(p * PW, (p + 1) * PW)
            state = state_ref[p]
            v_bd = block_diag(ops["v"][:, cols])
            o_p = _dot(probs[p], v_bd) + _dot(ops["q_dec"][:, cols], block_diag(state.astype(BF16)))
            k_pair = ops["k_end"][:, cols]
            k_rows = jnp.concatenate([k_pair[:, :HEAD_DIM], k_pair[:, HEAD_DIM:]], axis=0)
            kv = _dot_tn(k_rows, v_bd)
            dec = ops["decay"][:, cols]
            decay_col = jnp.concatenate(
                [jnp.broadcast_to(dec[:, h * HEAD_DIM:(h + 1) * HEAD_DIM], (HEAD_DIM, HEAD_DIM)).T for h in range(2)], axis=1)
            fill()
            state_ref[p] = decay_col * state + kv
            for half in range(2):
                o_h = o_p[:, half * HEAD_DIM:(half + 1) * HEAD_DIM]
                o_h = o_h * lax.rsqrt(jnp.mean(o_h * o_h, axis=-1, keepdims=True) + RMS_EPS)
                o_ref[rows, p * PW + half * HEAD_DIM:p * PW + (half + 1) * HEAD_DIM] = o_h

    def pool_branch():
        pv = jnp.concatenate([proj_out[(4, n)] for n in range(D_MODEL // FILL_COLS)], axis=-1)
        ext = jnp.concatenate([carry_ref[...], pv], axis=0)
        carry_ref[...] = pv[tm - POOL_HISTORY:, :]
        pos = j * tm + lax.broadcasted_iota(jnp.int32, (tm, 1), 0)
        b_parts = []
        for gi, w in enumerate(POOL_WINDOWS):
            cols = slice(gi * POOL_GROUP_DIM, (gi + 1) * POOL_GROUP_DIM)
            acc = ext[:, cols]
            span = 1
            while span < w:
                acc = acc + pltpu.roll(acc, span, 0)
                span *= 2
            inv_count = 1.0 / jnp.minimum(pos + 1, w).astype(F32)
            pooled = acc[POOL_HISTORY:, :] * inv_count - pv[:, cols]
            b_parts.append(_dot(pooled.astype(BF16), wp_ref[0, gi * POOL_GROUP_DIM:(gi + 1) * POOL_GROUP_DIM, :]))
        return jnp.concatenate(b_parts, axis=-1) * ps_ref[...]

    full = lambda c, rows: jnp.concatenate([proj_out[(c, n)][rows] for n in range(D_MODEL // FILL_COLS)], axis=-1)

    def tail(blk, b):
        rows = row_slices[blk]
        o = (o_ref[rows, :] * (0.5 * g_ref[...])) * (jnp.tanh(0.5 * full(3, rows)) + 1.0)
        a = _dot(o.astype(BF16), wa_ref[0])
        fill()
        merged = _sigmoid(full(5, rows)) * a + _sigmoid(full(6, rows)) * b[rows]
        mix = _dot(merged.astype(BF16), wo_ref[0])
        fill()
        y1_ref[rows, :] = alpha * x[rows] + mix

    ops = prepare(0)
    fill(6)
    recur(0, ops)
    b = pool_branch()
    ops = prepare(1)
    fill(2)
    tail(0, b)
    recur(1, ops)
    fill(len(units))

    mlp = jnp.concatenate([sum(d_parts[(c, m)] for c in range(n_ff)) for m in range(D_MODEL // FILL_COLS)], axis=-1)
    out_ref[...] = _layer_norm(alpha * x1 + mlp, l2g_ref[...], l2b_ref[...])
    tail(1, b)


def _const_spec(shape):
    nd = len(shape)
    return pl.BlockSpec(shape, lambda *_: (0,) * nd, pipeline_mode=pl.Buffered(1))


def _layer(x2d, seq_len, w_in, lb_logits, norm_g, w_a, w_pool, pool_scale, w_out, ln1_g, ln1_b,
           w_up, w_down, ln2_g, ln2_b, alpha):
    N, D = x2d.shape
    tm = TILE_TOKENS
    n_tiles = N // tm
    row = lambda p: p.reshape(1, D).astype(F32)
    hbm = pl.BlockSpec(memory_space=pl.ANY)
    return pl.pallas_call(
        functools.partial(_block_kernel, alpha, seq_len // tm),
        grid=(n_tiles + 1,),
        in_specs=[
            pl.BlockSpec((tm, D), lambda t: (jnp.minimum(t, n_tiles - 1), 0)),
            hbm,
            _const_spec(lb_logits.shape),
            _const_spec((1, D)),
            hbm,
            hbm,
            _const_spec((1, D)),
            hbm,
            _const_spec((1, D)),
            _const_spec((1, D)),
            hbm,
            hbm,
            _const_spec((1, D)),
            _const_spec((1, D)),
        ],
        out_specs=pl.BlockSpec((tm, D), lambda t: (jnp.maximum(t - 1, 0), 0)),
        out_shape=jax.ShapeDtypeStruct((N, D), F32),
        scratch_shapes=[
            pltpu.VMEM((HEADS // 2, HEAD_DIM, 2 * HEAD_DIM), F32),
            pltpu.VMEM((POOL_HISTORY, D), F32),
            pltpu.VMEM((tm, D), F32),
            pltpu.VMEM((tm, D), F32),
            pltpu.VMEM((IN_COLS // D, D, D), BF16),
            pltpu.VMEM((1, D, D), BF16),
            pltpu.VMEM((1, D, POOL_GROUP_DIM), BF16),
            pltpu.VMEM((1, D, D), BF16),
            pltpu.VMEM((D_FF // D, D, D), BF16),
            pltpu.VMEM((D_FF // D, D, D), BF16),
            pltpu.VMEM((LOAD_SLOTS, LOAD_ROWS, D), F32),
            pltpu.SemaphoreType.DMA((LOAD_SLOTS,)),
        ],
        compiler_params=pltpu.CompilerParams(
            dimension_semantics=("arbitrary",),
            vmem_limit_bytes=VMEM_LIMIT_BYTES),
        name="hgrn2_pool_mlp_block",
    )(x2d, w_in, lb_logits.astype(F32), row(norm_g), w_a, w_pool.reshape(D, POOL_GROUP_DIM),
      row(pool_scale), w_out, row(ln1_g), row(ln1_b), w_up, w_down, row(ln2_g), row(ln2_b))


def kernel(x, w_in, lb_logits, hgrn_norm_g, w_a, w_pool, pool_scale, w_out, ln1_g, ln1_b, w_up, w_down, ln2_g, ln2_b):
    depth = w_in.shape[0]
    assert depth == 1 and lb_logits.shape[0] == depth + 1
    alpha = (2.0 * depth) ** 0.25
    B, S, D = x.shape
    x2d = x.reshape(B * S, D)
    for l in range(depth):
        x2d = _layer(x2d, S, w_in[l], lb_logits, hgrn_norm_g[l], w_a[l], w_pool[l], pool_scale[l], w_out[l],
                     ln1_g[l], ln1_b[l], w_up[l], w_down[l], ln2_g[l], ln2_b[l], alpha)
    return x2d.reshape(B, S, D)
```

```python
import functools

import jax
import jax.numpy as jnp
from jax import lax
from jax.experimental import pallas as pl
from jax.experimental.pallas import tpu as pltpu

D_MODEL = 1024
HEADS = 8
HEAD_DIM = D_MODEL // HEADS
POOL_WINDOWS = (2, 4, 8, 16)
POOL_GROUP_DIM = D_MODEL // len(POOL_WINDOWS)
POOL_HISTORY = 16
D_FF = 4 * D_MODEL
IN_COLS = 7 * D_MODEL
LN_EPS = 1e-5
RMS_EPS = 1e-6

REC_BLOCK = 128
DIAG_CHUNK = 32
TILE_TOKENS = 256
FILL_COLS = 512
FF_CHUNK = 1024
LOAD_ROWS = 256
LOAD_SLOTS = 5
VMEM_LIMIT_BYTES = 60 * 1024 * 1024

BF16 = jnp.bfloat16
F32 = jnp.float32


def _sigmoid(z):
    return 0.5 * jnp.tanh(0.5 * z) + 0.5


def _dot(a, b):
    return jnp.dot(a, b, preferred_element_type=F32)


def _dot_nt(a, b):
    return lax.dot_general(a, b, (((1,), (1,)), ((), ())), preferred_element_type=F32)


def _dot_tn(a, b):
    return lax.dot_general(a, b, (((0,), (0,)), ((), ())), preferred_element_type=F32)


def _layer_norm(y, g, b):
    mu = jnp.mean(y, axis=-1, keepdims=True)
    c = y - mu
    var = jnp.mean(c * c, axis=-1, keepdims=True)
    return c * lax.rsqrt(var + LN_EPS) * g + b


def _block_cumsum(tri2, z):
    z1 = z.astype(BF16)
    z2 = (z - z1.astype(F32)).astype(BF16)
    return _dot(tri2, jnp.concatenate([z1, z2], axis=0))


def _row_bcast(g, period, row):
    n = g.shape[0] // period
    g3 = g.reshape(n, period, g.shape[1])
    r = jnp.broadcast_to(g3[:, row:row + 1, :], g3.shape)
    return r.reshape(g.shape)


def _load_weights(weights, stage, sem):
    ahead = LOAD_SLOTS - 1
    counts = []
    for _, dst in weights:
        n_blocks, n_rows, _ = dst.shape
        assert n_rows % LOAD_ROWS == 0
        counts.append(n_blocks * (n_rows // LOAD_ROWS))
    assert min(counts) >= ahead
    starts = [sum(counts[:w]) for w in range(len(weights))]

    def copy(w, i):
        src_chunk, dst = weights[w]
        per_block = dst.shape[1] // LOAD_ROWS
        r0 = pl.multiple_of((i % per_block) * LOAD_ROWS, LOAD_ROWS)
        slot = (starts[w] + i) % LOAD_SLOTS
        return pltpu.make_async_copy(src_chunk(i // per_block, r0), stage.at[slot, :, pl.ds(0, dst.shape[2])], sem.at[slot])

    for i in range(ahead):
        copy(0, i).start()

    for w, (_, dst) in enumerate(weights):
        n = counts[w]
        per_block = dst.shape[1] // LOAD_ROWS
        cols = dst.shape[2]

        def body(i, carry, w=w, n=n, per_block=per_block, cols=cols, dst=dst):
            @pl.when(i + ahead < n)
            def _():
                copy(w, i + ahead).start()

            if w + 1 < len(weights):
                @pl.when(i + ahead >= n)
                def _():
                    copy(w + 1, i + ahead - n).start()

            copy(w, i).wait()
            r0 = pl.multiple_of((i % per_block) * LOAD_ROWS, LOAD_ROWS)
            slot = (starts[w] + i) % LOAD_SLOTS
            dst[i // per_block, pl.ds(r0, LOAD_ROWS), :] = stage[slot, :, pl.ds(0, cols)].astype(BF16)
            return carry

        lax.fori_loop(0, n, body, 0)


def _block_kernel(alpha, seq_tiles, x_ref, w_in_hbm, lbl_ref, g_ref, wa_hbm, wp_hbm, ps_ref, wo_hbm,
                  l1g_ref, l1b_ref, wu_hbm, wd_hbm, l2g_ref, l2b_ref, out_ref,
                  state_ref, carry_ref, o_ref, y1_ref,
                  w_in_ref, wa_ref, wp_ref, wo_ref, wu_ref, wd_ref, stage_ref, load_sem):
    tm = x_ref.shape[0]
    t = pl.program_id(0)
    j = t % seq_tiles

    @pl.when(t == 0)
    def _():
        y1_ref[...] = jnp.zeros_like(y1_ref)
        rows = lambda r0: pl.ds(r0, LOAD_ROWS)
        col_block = lambda hbm: (lambda b, r0: hbm.at[rows(r0), pl.ds(pl.multiple_of(b * D_MODEL, D_MODEL), D_MODEL)])
        row_block = lambda hbm: (lambda b, r0: hbm.at[rows(pl.multiple_of(b * D_MODEL, D_MODEL) + r0), :])
        _load_weights([(col_block(w_in_hbm), w_in_ref), (row_block(wa_hbm), wa_ref), (row_block(wp_hbm), wp_ref),
                       (row_block(wo_hbm), wo_ref), (col_block(wu_hbm), wu_ref), (row_block(wd_hbm), wd_ref)],
                      stage_ref, load_sem)

    @pl.when(j == 0)
    def _():
        state_ref[...] = jnp.zeros_like(state_ref)
        carry_ref[...] = jnp.zeros_like(carry_ref)

    x = x_ref[...]
    xb = x.astype(BF16)

    def proj(c):
        return _dot(xb, w_in_ref[c])

    l0 = lbl_ref[0:1, :]
    l1 = lbl_ref[1:2, :]
    lmax = jnp.maximum(l0, l1)
    e0 = jnp.exp(l0 - lmax)
    e1 = jnp.exp(l1 - lmax)
    lb = e0 / (e0 + e1)
    f_mid = 0.5 * (1.0 + lb)
    f_half = 0.5 * (1.0 - lb)

    f_swing = f_half * jnp.tanh(0.5 * proj(1))
    k = f_half - f_swing
    logf = jnp.log(f_mid + f_swing)
    q = proj(0)
    q = (q * (0.5 * HEAD_DIM ** -0.5)) * (jnp.tanh(0.5 * q) + 1.0)
    v = proj(2).astype(BF16)

    L = REC_BLOCK
    assert tm == 2 * L
    PW = 2 * HEAD_DIM
    ti = lax.broadcasted_iota(jnp.int32, (L, 2 * L), 0)
    si = lax.broadcasted_iota(jnp.int32, (L, 2 * L), 1)
    tri2 = ((si % L) <= ti).astype(BF16)
    tp = lax.broadcasted_iota(jnp.int32, (L, PW), 0)
    lp = lax.broadcasted_iota(jnp.int32, (L, PW), 1)
    sp = lp % L
    m_diag = ((tp // DIAG_CHUNK) == (sp // DIAG_CHUNK)) & (sp <= tp)
    m_32 = ((tp // 64) == (sp // 64)) & ((tp % 64) >= 32) & ((sp % 64) < 32)
    m_64 = (tp >= 64) & (sp < 64)
    head_a = lp < HEAD_DIM

    def block_diag(pair):
        z = jnp.zeros_like(pair)
        return jnp.concatenate([jnp.where(head_a, pair, z), jnp.where(head_a, z, pair)], axis=0)

    n_blk = tm // L
    row_slices = [slice(i * L, (i + 1) * L) for i in range(n_blk)]
    cums = [_block_cumsum(tri2, logf[rows]) for rows in row_slices]

    def prepare(blk):
        rows = row_slices[blk]
        qb, kb, G = q[rows], k[rows], cums[blk]
        g_last = G[L - 1:L, :]
        ops = dict(
            v=v[rows],
            q_dec=(qb * jnp.exp(G)).astype(BF16),
            k_end=(kb * jnp.exp(g_last - G)).astype(BF16),
            decay=jnp.exp(g_last))
        r_d = _row_bcast(G, DIAG_CHUNK, DIAG_CHUNK // 2 - 1)
        ops["q_d"] = (qb * jnp.exp(G - r_d)).astype(BF16)
        ops["k_d"] = (kb * jnp.exp(r_d - G)).astype(BF16)
        e_32 = jnp.exp(-jnp.abs(G - _row_bcast(G, 64, 31)))
        ops["q_32"] = (qb * e_32).astype(BF16)
        ops["k_32"] = (kb * e_32).astype(BF16)
        e_64 = jnp.exp(-jnp.abs(G - _row_bcast(G, 128, 63)))
        ops["q_64"] = (qb * e_64).astype(BF16)
        ops["k_64"] = (kb * e_64).astype(BF16)
        return ops

    units = []
    proj_out = {}

    def proj_unit(c, n):
        def run():
            proj_out[(c, n)] = _dot(xb, w_in_ref[c, :, n * FILL_COLS:(n + 1) * FILL_COLS])
        return run

    x1 = _layer_norm(y1_ref[...], l1g_ref[...], l1b_ref[...])
    x1b = x1.astype(BF16)
    n_ff = D_FF // FF_CHUNK
    assert FF_CHUNK == D_MODEL
    h_parts, d_parts = {}, {}

    def up_unit(c, n):
        def run():
            hh = jnp.maximum(_dot(x1b, wu_ref[c, :, n * FILL_COLS:(n + 1) * FILL_COLS]), 0.0)
            h_parts[(c, n)] = (hh * hh).astype(BF16)
        return run

    def down_unit(c, m):
        def run():
            hc = jnp.concatenate([h_parts[(c, n)] for n in range(FF_CHUNK // FILL_COLS)], axis=-1)
            d_parts[(c, m)] = _dot(hc, wd_ref[c, :, m * FILL_COLS:(m + 1) * FILL_COLS])
        return run

    ups = [[up_unit(c, n) for n in range(FF_CHUNK // FILL_COLS)] for c in range(n_ff)]
    downs = [[down_unit(c, m) for m in range(D_MODEL // FILL_COLS)] for c in range(n_ff)]
    projs = [[proj_unit(c, n) for n in range(D_MODEL // FILL_COLS)] for c in (3, 4, 5, 6)]
    units += projs[0] + projs[1] + projs[2] + projs[3]
    units += ups[0] + ups[1] + downs[0] + ups[2] + downs[1] + ups[3] + downs[2] + downs[3]

    def fill(n=1):
        for _ in range(n):
            if units:
                units.pop(0)()

    def recur(blk, ops):
        rows = row_slices[blk]
        probs = []
        for p in range(HEADS // 2):
            cols = slice(p * PW, (p + 1) * PW)
            s_d = _dot_nt(ops["q_d"][:, cols], block_diag(ops["k_d"][:, cols]))
            s_32 = _dot_nt(ops["q_32"][:, cols], block_diag(ops["k_32"][:, cols]))
            s_64 = _dot_nt(ops["q_64"][:, cols], block_diag(ops["k_64"][:, cols]))
            fill()
            probs.append(jnp.where(m_diag, s_d, jnp.where(m_32, s_32, jnp.where(m_64, s_64, 0.0))).astype(BF16))
        for p in range(HEADS // 2):
            cols = slice(p * PW, (p + 1) * PW)
            state = state_ref[p]
            v_bd = block_diag(ops["v"][:, cols])
            o_p = _dot(jnp.concatenate([probs[p], ops["q_dec"][:, cols]], axis=1),
                       jnp.concatenate([v_bd, block_diag(state.astype(BF16))], axis=0))
            k_pair = ops["k_end"][:, cols]
            k_rows = jnp.concatenate([k_pair[:, :HEAD_DIM], k_pair[:, HEAD_DIM:]], axis=0)
            kv = _dot_tn(k_rows, v_bd)
            dec = ops["decay"][:, cols]
            decay_col = jnp.concatenate(
                [jnp.broadcast_to(dec[:, h * HEAD_DIM:(h + 1) * HEAD_DIM], (HEAD_DIM, HEAD_DIM)).T for h in range(2)], axis=1)
            fill()
            state_ref[p] = decay_col * state + kv
            for half in range(2):
                o_h = o_p[:, half * HEAD_DIM:(half + 1) * HEAD_DIM]
                o_h = o_h * lax.rsqrt(jnp.mean(o_h * o_h, axis=-1, keepdims=True) + RMS_EPS)
                o_ref[rows, p * PW + half * HEAD_DIM:p * PW + (half + 1) * HEAD_DIM] = o_h

    def pool_branch():
        pv = jnp.concatenate([proj_out[(4, n)] for n in range(D_MODEL // FILL_COLS)], axis=-1)
        ext = jnp.concatenate([carry_ref[...], pv], axis=0)
        carry_ref[...] = pv[tm - POOL_HISTORY:, :]
        pos = j * tm + lax.broadcasted_iota(jnp.int32, (tm, 1), 0)
        b_parts = []
        for gi, w in enumerate(POOL_WINDOWS):
            cols = slice(gi * POOL_GROUP_DIM, (gi + 1) * POOL_GROUP_DIM)
            acc = ext[:, cols]
            span = 1
            while span < w:
                acc = acc + pltpu.roll(acc, span, 0)
                span *= 2
            inv_count = 1.0 / jnp.minimum(pos + 1, w).astype(F32)
            pooled = acc[POOL_HISTORY:, :] * inv_count - pv[:, cols]
            b_parts.append(_dot(pooled.astype(BF16), wp_ref[0, gi * POOL_GROUP_DIM:(gi + 1) * POOL_GROUP_DIM, :]))
        return jnp.concatenate(b_parts, axis=-1) * ps_ref[...]

    full = lambda c, rows: jnp.concatenate([proj_out[(c, n)][rows] for n in range(D_MODEL // FILL_COLS)], axis=-1)

    def tail(blk, b):
        rows = row_slices[blk]
        o = (o_ref[rows, :] * (0.5 * g_ref[...])) * (jnp.tanh(0.5 * full(3, rows)) + 1.0)
        a = _dot(o.astype(BF16), wa_ref[0])
        fill()
        merged = _sigmoid(full(5, rows)) * a + _sigmoid(full(6, rows)) * b[rows]
        mix = _dot(merged.astype(BF16), wo_ref[0])
        fill()
        y1_ref[rows, :] = alpha * x[rows] + mix

    ops = prepare(0)
    fill(8)
    recur(0, ops)
    b = pool_branch()
    ops = prepare(1)
    fill(2)
    tail(0, b)
    recur(1, ops)
    fill(len(units))

    mlp = jnp.concatenate([sum(d_parts[(c, m)] for c in range(n_ff)) for m in range(D_MODEL // FILL_COLS)], axis=-1)
    out_ref[...] = _layer_norm(alpha * x1 + mlp, l2g_ref[...], l2b_ref[...])
    tail(1, b)


def _const_spec(shape):
    nd = len(shape)
    return pl.BlockSpec(shape, lambda *_: (0,) * nd, pipeline_mode=pl.Buffered(1))


def _layer(x2d, seq_len, w_in, lb_logits, norm_g, w_a, w_pool, pool_scale, w_out, ln1_g, ln1_b,
           w_up, w_down, ln2_g, ln2_b, alpha):
    N, D = x2d.shape
    tm = TILE_TOKENS
    n_tiles = N // tm
    row = lambda p: p.reshape(1, D).astype(F32)
    hbm = pl.BlockSpec(memory_space=pl.ANY)
    return pl.pallas_call(
        functools.partial(_block_kernel, alpha, seq_len // tm),
        grid=(n_tiles + 1,),
        in_specs=[
            pl.BlockSpec((tm, D), lambda t: (jnp.minimum(t, n_tiles - 1), 0)),
            hbm,
            _const_spec(lb_logits.shape),
            _const_spec((1, D)),
            hbm,
            hbm,
            _const_spec((1, D)),
            hbm,
            _const_spec((1, D)),
            _const_spec((1, D)),
            hbm,
            hbm,
            _const_spec((1, D)),
            _const_spec((1, D)),
        ],
        out_specs=pl.BlockSpec((tm, D), lambda t: (jnp.maximum(t - 1, 0), 0)),
        out_shape=jax.ShapeDtypeStruct((N, D), F32),
        scratch_shapes=[
            pltpu.VMEM((HEADS // 2, HEAD_DIM, 2 * HEAD_DIM), F32),
            pltpu.VMEM((POOL_HISTORY, D), F32),
            pltpu.VMEM((tm, D), F32),
            pltpu.VMEM((tm, D), F32),
            pltpu.VMEM((IN_COLS // D, D, D), BF16),
            pltpu.VMEM((1, D, D), BF16),
            pltpu.VMEM((1, D, POOL_GROUP_DIM), BF16),
            pltpu.VMEM((1, D, D), BF16),
            pltpu.VMEM((D_FF // D, D, D), BF16),
            pltpu.VMEM((D_FF // D, D, D), BF16),
            pltpu.VMEM((LOAD_SLOTS, LOAD_ROWS, D), F32),
            pltpu.SemaphoreType.DMA((LOAD_SLOTS,)),
        ],
        compiler_params=pltpu.CompilerParams(
            dimension_semantics=("arbitrary",),
            vmem_limit_bytes=VMEM_LIMIT_BYTES),
        name="hgrn2_pool_mlp_block",
    )(x2d, w_in, lb_logits.astype(F32), row(norm_g), w_a, w_pool.reshape(D, POOL_GROUP_DIM),
      row(pool_scale), w_out, row(ln1_g), row(ln1_b), w_up, w_down, row(ln2_g), row(ln2_b))


def kernel(x, w_in, lb_logits, hgrn_norm_g, w_a, w_pool, pool_scale, w_out, ln1_g, ln1_b, w_up, w_down, ln2_g, ln2_b):
    depth = w_in.shape[0]
    assert depth == 1 and lb_logits.shape[0] == depth + 1
    alpha = (2.0 * depth) ** 0.25
    B, S, D = x.shape
    x2d = x.reshape(B * S, D)
    for l in range(depth):
        x2d = _layer(x2d, S, w_in[l], lb_logits, hgrn_norm_g[l], w_a[l], w_pool[l], pool_scale[l], w_out[l],
                     ln1_g[l], ln1_b[l], w_up[l], w_down[l], ln2_g[l], ln2_b[l], alpha)
    return x2d.reshape(B, S, D)
```

```python
import functools

import jax
import jax.numpy as jnp
from jax import lax
from jax.experimental import pallas as pl
from jax.experimental.pallas import tpu as pltpu

D_MODEL = 1024
HEADS = 8
HEAD_DIM = D_MODEL // HEADS
POOL_WINDOWS = (2, 4, 8, 16)
POOL_GROUP_DIM = D_MODEL // len(POOL_WINDOWS)
POOL_HISTORY = 16
D_FF = 4 * D_MODEL
IN_COLS = 7 * D_MODEL
LN_EPS = 1e-5
RMS_EPS = 1e-6

REC_BLOCK = 128
DIAG_CHUNK = 32
TILE_TOKENS = 256
FILL_COLS = 512
FF_CHUNK = 1024
LOAD_ROWS = 256
LOAD_SLOTS = 5
VMEM_LIMIT_BYTES = 60 * 1024 * 1024

BF16 = jnp.bfloat16
F32 = jnp.float32


def _sigmoid(z):
    return 0.5 * jnp.tanh(0.5 * z) + 0.5


def _dot(a, b):
    return jnp.dot(a, b, preferred_element_type=F32)


def _dot_nt(a, b):
    return lax.dot_general(a, b, (((1,), (1,)), ((), ())), preferred_element_type=F32)


def _dot_tn(a, b):
    return lax.dot_general(a, b, (((0,), (0,)), ((), ())), preferred_element_type=F32)


def _layer_norm(y, g, b):
    mu = jnp.mean(y, axis=-1, keepdims=True)
    c = y - mu
    var = jnp.mean(c * c, axis=-1, keepdims=True)
    return c * lax.rsqrt(var + LN_EPS) * g + b


def _block_cumsum(tri2, z):
    z1 = z.astype(BF16)
    z2 = (z - z1.astype(F32)).astype(BF16)
    return _dot(tri2, jnp.concatenate([z1, z2], axis=0))


def _row_bcast(g, period, row):
    n = g.shape[0] // period
    g3 = g.reshape(n, period, g.shape[1])
    r = jnp.broadcast_to(g3[:, row:row + 1, :], g3.shape)
    return r.reshape(g.shape)


def _load_weights(weights, stage, sem):
    ahead = LOAD_SLOTS - 1
    counts = []
    for _, dst in weights:
        n_blocks, n_rows, _ = dst.shape
        assert n_rows % LOAD_ROWS == 0
        counts.append(n_blocks * (n_rows // LOAD_ROWS))
    assert min(counts) >= ahead
    starts = [sum(counts[:w]) for w in range(len(weights))]

    def copy(w, i):
        src_chunk, dst = weights[w]
        per_block = dst.shape[1] // LOAD_ROWS
        r0 = pl.multiple_of((i % per_block) * LOAD_ROWS, LOAD_ROWS)
        slot = (starts[w] + i) % LOAD_SLOTS
        return pltpu.make_async_copy(src_chunk(i // per_block, r0), stage.at[slot, :, pl.ds(0, dst.shape[2])], sem.at[slot])

    for i in range(ahead):
        copy(0, i).start()

    for w, (_, dst) in enumerate(weights):
        n = counts[w]
        per_block = dst.shape[1] // LOAD_ROWS
        cols = dst.shape[2]

        def body(i, carry, w=w, n=n, per_block=per_block, cols=cols, dst=dst):
            @pl.when(i + ahead < n)
            def _():
                copy(w, i + ahead).start()

            if w + 1 < len(weights):
                @pl.when(i + ahead >= n)
                def _():
                    copy(w + 1, i + ahead - n).start()

            copy(w, i).wait()
            r0 = pl.multiple_of((i % per_block) * LOAD_ROWS, LOAD_ROWS)
            slot = (starts[w] + i) % LOAD_SLOTS
            dst[i // per_block, pl.ds(r0, LOAD_ROWS), :] = stage[slot, :, pl.ds(0, cols)].astype(BF16)
            return carry

        lax.fori_loop(0, n, body, 0)


def _block_kernel(alpha, seq_tiles, x_ref, w_in_hbm, lbl_ref, g_ref, wa_hbm, wp_hbm, ps_ref, wo_hbm,
                  l1g_ref, l1b_ref, wu_hbm, wd_hbm, l2g_ref, l2b_ref, out_ref,
                  state_ref, carry_ref, o_ref, y1_ref,
                  w_in_ref, wa_ref, wp_ref, wo_ref, wu_ref, wd_ref, stage_ref, load_sem):
    tm = x_ref.shape[0]
    t = pl.program_id(0)
    j = t % seq_tiles

    @pl.when(t == 0)
    def _():
        y1_ref[...] = jnp.zeros_like(y1_ref)
        rows = lambda r0: pl.ds(r0, LOAD_ROWS)
        col_block = lambda hbm: (lambda b, r0: hbm.at[rows(r0), pl.ds(pl.multiple_of(b * D_MODEL, D_MODEL), D_MODEL)])
        row_block = lambda hbm: (lambda b, r0: hbm.at[rows(pl.multiple_of(b * D_MODEL, D_MODEL) + r0), :])
        _load_weights([(col_block(w_in_hbm), w_in_ref), (row_block(wa_hbm), wa_ref), (row_block(wp_hbm), wp_ref),
                       (row_block(wo_hbm), wo_ref), (col_block(wu_hbm), wu_ref), (row_block(wd_hbm), wd_ref)],
                      stage_ref, load_sem)

    @pl.when(j == 0)
    def _():
        state_ref[...] = jnp.zeros_like(state_ref)
        carry_ref[...] = jnp.zeros_like(carry_ref)

    x = x_ref[...]
    xb = x.astype(BF16)

    def proj(c):
        return _dot(xb, w_in_ref[c])

    l0 = lbl_ref[0:1, :]
    l1 = lbl_ref[1:2, :]
    lmax = jnp.maximum(l0, l1)
    e0 = jnp.exp(l0 - lmax)
    e1 = jnp.exp(l1 - lmax)
    lb = e0 / (e0 + e1)
    f_mid = 0.5 * (1.0 + lb)
    f_half = 0.5 * (1.0 - lb)

    f_swing = f_half * jnp.tanh(0.5 * proj(1))
    k = f_half - f_swing
    logf = jnp.log(f_mid + f_swing)
    q = proj(0)
    q = (q * (0.5 * HEAD_DIM ** -0.5)) * (jnp.tanh(0.5 * q) + 1.0)
    v = proj(2).astype(BF16)

    L = REC_BLOCK
    assert tm == 2 * L
    PW = 2 * HEAD_DIM
    ti = lax.broadcasted_iota(jnp.int32, (L, 2 * L), 0)
    si = lax.broadcasted_iota(jnp.int32, (L, 2 * L), 1)
    tri2 = ((si % L) <= ti).astype(BF16)
    tp = lax.broadcasted_iota(jnp.int32, (L, PW), 0)
    lp = lax.broadcasted_iota(jnp.int32, (L, PW), 1)
    sp = lp % L
    m_diag = ((tp // DIAG_CHUNK) == (sp // DIAG_CHUNK)) & (sp <= tp)
    m_32 = ((tp // 64) == (sp // 64)) & ((tp % 64) >= 32) & ((sp % 64) < 32)
    m_64 = (tp >= 64) & (sp < 64)
    head_a = lp < HEAD_DIM

    def block_diag(pair):
        z = jnp.zeros_like(pair)
        return jnp.concatenate([jnp.where(head_a, pair, z), jnp.where(head_a, z, pair)], axis=0)

    n_blk = tm // L
    row_slices = [slice(i * L, (i + 1) * L) for i in range(n_blk)]
    cums = [_block_cumsum(tri2, logf[rows]) for rows in row_slices]

    def prepare(blk):
        rows = row_slices[blk]
        qb, kb, G = q[rows], k[rows], cums[blk]
        g_last = G[L - 1:L, :]
        ops = dict(
            v=v[rows],
            q_dec=(qb * jnp.exp(G)).astype(BF16),
            k_end=(kb * jnp.exp(g_last - G)).astype(BF16),
            decay=jnp.broadcast_to(jnp.exp(g_last), (8, D_MODEL)).T)
        r_d = _row_bcast(G, DIAG_CHUNK, DIAG_CHUNK // 2 - 1)
        ops["q_d"] = (qb * jnp.exp(G - r_d)).astype(BF16)
        ops["k_d"] = (kb * jnp.exp(r_d - G)).astype(BF16)
        e_32 = jnp.exp(-jnp.abs(G - _row_bcast(G, 64, 31)))
        ops["q_32"] = (qb * e_32).astype(BF16)
        ops["k_32"] = (kb * e_32).astype(BF16)
        e_64 = jnp.exp(-jnp.abs(G - _row_bcast(G, 128, 63)))
        ops["q_64"] = (qb * e_64).astype(BF16)
        ops["k_64"] = (kb * e_64).astype(BF16)
        return ops

    units = []
    proj_out = {}

    def proj_unit(c, n):
        def run():
            proj_out[(c, n)] = _dot(xb, w_in_ref[c, :, n * FILL_COLS:(n + 1) * FILL_COLS])
        return run

    x1 = _layer_norm(y1_ref[...], l1g_ref[...], l1b_ref[...])
    x1b = x1.astype(BF16)
    n_ff = D_FF // FF_CHUNK
    assert FF_CHUNK == D_MODEL
    h_parts, d_parts = {}, {}

    def up_unit(c, n):
        def run():
            hh = jnp.maximum(_dot(x1b, wu_ref[c, :, n * FILL_COLS:(n + 1) * FILL_COLS]), 0.0)
            h_parts[(c, n)] = (hh * hh).astype(BF16)
        return run

    def down_unit(c, m):
        def run():
            hc = jnp.concatenate([h_parts[(c, n)] for n in range(FF_CHUNK // FILL_COLS)], axis=-1)
            d_parts[(c, m)] = _dot(hc, wd_ref[c, :, m * FILL_COLS:(m + 1) * FILL_COLS])
        return run

    ups = [[up_unit(c, n) for n in range(FF_CHUNK // FILL_COLS)] for c in range(n_ff)]
    downs = [[down_unit(c, m) for m in range(D_MODEL // FILL_COLS)] for c in range(n_ff)]
    projs = [[proj_unit(c, n) for n in range(D_MODEL // FILL_COLS)] for c in (3, 4, 5, 6)]
    units += projs[0] + projs[1] + projs[2] + projs[3]
    units += ups[0] + ups[1] + downs[0] + ups[2] + downs[1] + ups[3] + downs[2] + downs[3]

    def fill(n=1):
        for _ in range(n):
            if units:
                units.pop(0)()

    def recur(blk, ops):
        rows = row_slices[blk]
        probs = []
        for p in range(HEADS // 2):
            cols = slice(p * PW, (p + 1) * PW)
            s_d = _dot_nt(ops["q_d"][:, cols], block_diag(ops["k_d"][:, cols]))
            s_32 = _dot_nt(ops["q_32"][:, cols], block_diag(ops["k_32"][:, cols]))
            s_64 = _dot_nt(ops["q_64"][:, cols], block_diag(ops["k_64"][:, cols]))
            fill()
            probs.append(jnp.where(m_diag, s_d, jnp.where(m_32, s_32, jnp.where(m_64, s_64, 0.0))).astype(BF16))
        for p in range(HEADS // 2):
            cols = slice(p * PW, (p + 1) * PW)
            state = state_ref[p]
            v_bd = block_diag(ops["v"][:, cols])
            o_p = _dot(jnp.concatenate([probs[p], ops["q_dec"][:, cols]], axis=1),
                       jnp.concatenate([v_bd, block_diag(state.astype(BF16))], axis=0))
            k_pair = ops["k_end"][:, cols]
            k_rows = jnp.concatenate([k_pair[:, :HEAD_DIM], k_pair[:, HEAD_DIM:]], axis=0)
            kv = _dot_tn(k_rows, v_bd)
            decay_col = jnp.concatenate(
                [jnp.broadcast_to(ops["decay"][(2 * p + h) * HEAD_DIM:(2 * p + h + 1) * HEAD_DIM, 0:1], (HEAD_DIM, HEAD_DIM))
                 for h in range(2)], axis=1)
            fill()
            state_ref[p] = decay_col * state + kv
            for half in range(2):
                o_h = o_p[:, half * HEAD_DIM:(half + 1) * HEAD_DIM]
                o_h = o_h * lax.rsqrt(jnp.mean(o_h * o_h, axis=-1, keepdims=True) + RMS_EPS)
                o_ref[rows, p * PW + half * HEAD_DIM:p * PW + (half + 1) * HEAD_DIM] = o_h

    def pool_branch():
        pv = jnp.concatenate([proj_out[(4, n)] for n in range(D_MODEL // FILL_COLS)], axis=-1)
        ext = jnp.concatenate([carry_ref[...], pv], axis=0)
        carry_ref[...] = pv[tm - POOL_HISTORY:, :]
        pos = j * tm + lax.broadcasted_iota(jnp.int32, (tm, 1), 0)
        b_parts = []
        for gi, w in enumerate(POOL_WINDOWS):
            cols = slice(gi * POOL_GROUP_DIM, (gi + 1) * POOL_GROUP_DIM)
            acc = ext[:, cols]
            span = 1
            while span < w:
                acc = acc + pltpu.roll(acc, span, 0)
                span *= 2
            inv_count = 1.0 / jnp.minimum(pos + 1, w).astype(F32)
            pooled = acc[POOL_HISTORY:, :] * inv_count - pv[:, cols]
            b_parts.append(_dot(pooled.astype(BF16), wp_ref[0, gi * POOL_GROUP_DIM:(gi + 1) * POOL_GROUP_DIM, :]))
        return jnp.concatenate(b_parts, axis=-1) * ps_ref[...]

    full = lambda c, rows: jnp.concatenate([proj_out[(c, n)][rows] for n in range(D_MODEL // FILL_COLS)], axis=-1)

    def tail(blk, b):
        rows = row_slices[blk]
        o = (o_ref[rows, :] * (0.5 * g_ref[...])) * (jnp.tanh(0.5 * full(3, rows)) + 1.0)
        a = _dot(o.astype(BF16), wa_ref[0])
        fill()
        merged = _sigmoid(full(5, rows)) * a + _sigmoid(full(6, rows)) * b[rows]
        mix = _dot(merged.astype(BF16), wo_ref[0])
        fill()
        y1_ref[rows, :] = alpha * x[rows] + mix

    ops = prepare(0)
    fill(8)
    recur(0, ops)
    b = pool_branch()
    ops = prepare(1)
    fill(2)
    tail(0, b)
    recur(1, ops)
    fill(len(units))

    mlp = jnp.concatenate([sum(d_parts[(c, m)] for c in range(n_ff)) for m in range(D_MODEL // FILL_COLS)], axis=-1)
    out_ref[...] = _layer_norm(alpha * x1 + mlp, l2g_ref[...], l2b_ref[...])
    tail(1, b)


def _const_spec(shape):
    nd = len(shape)
    return pl.BlockSpec(shape, lambda *_: (0,) * nd, pipeline_mode=pl.Buffered(1))


def _layer(x2d, seq_len, w_in, lb_logits, norm_g, w_a, w_pool, pool_scale, w_out, ln1_g, ln1_b,
           w_up, w_down, ln2_g, ln2_b, alpha):
    N, D = x2d.shape
    tm = TILE_TOKENS
    n_tiles = N // tm
    row = lambda p: p.reshape(1, D).astype(F32)
    hbm = pl.BlockSpec(memory_space=pl.ANY)
    return pl.pallas_call(
        functools.partial(_block_kernel, alpha, seq_len // tm),
        grid=(n_tiles + 1,),
        in_specs=[
            pl.BlockSpec((tm, D), lambda t: (jnp.minimum(t, n_tiles - 1), 0)),
            hbm,
            _const_spec(lb_logits.shape),
            _const_spec((1, D)),
            hbm,
            hbm,
            _const_spec((1, D)),
            hbm,
            _const_spec((1, D)),
            _const_spec((1, D)),
            hbm,
            hbm,
            _const_spec((1, D)),
            _const_spec((1, D)),
        ],
        out_specs=pl.BlockSpec((tm, D), lambda t: (jnp.maximum(t - 1, 0), 0)),
        out_shape=jax.ShapeDtypeStruct((N, D), F32),
        scratch_shapes=[
            pltpu.VMEM((HEADS // 2, HEAD_DIM, 2 * HEAD_DIM), F32),
            pltpu.VMEM((POOL_HISTORY, D), F32),
            pltpu.VMEM((tm, D), F32),
            pltpu.VMEM((tm, D), F32),
            pltpu.VMEM((IN_COLS // D, D, D), BF16),
            pltpu.VMEM((1, D, D), BF16),
            pltpu.VMEM((1, D, POOL_GROUP_DIM), BF16),
            pltpu.VMEM((1, D, D), BF16),
            pltpu.VMEM((D_FF // D, D, D), BF16),
            pltpu.VMEM((D_FF // D, D, D), BF16),
            pltpu.VMEM((LOAD_SLOTS, LOAD_ROWS, D), F32),
            pltpu.SemaphoreType.DMA((LOAD_SLOTS,)),
        ],
        compiler_params=pltpu.CompilerParams(
            dimension_semantics=("arbitrary",),
            vmem_limit_bytes=VMEM_LIMIT_BYTES),
        name="hgrn2_pool_mlp_block",
    )(x2d, w_in, lb_logits.astype(F32), row(norm_g), w_a, w_pool.reshape(D, POOL_GROUP_DIM),
      row(pool_scale), w_out, row(ln1_g), row(ln1_b), w_up, w_down, row(ln2_g), row(ln2_b))


def kernel(x, w_in, lb_logits, hgrn_norm_g, w_a, w_pool, pool_scale, w_out, ln1_g, ln1_b, w_up, w_down, ln2_g, ln2_b):
    depth = w_in.shape[0]
    assert depth == 1 and lb_logits.shape[0] == depth + 1
    alpha = (2.0 * depth) ** 0.25
    B, S, D = x.shape
    x2d = x.reshape(B * S, D)
    for l in range(depth):
        x2d = _layer(x2d, S, w_in[l], lb_logits, hgrn_norm_g[l], w_a[l], w_pool[l], pool_scale[l], w_out[l],
                     ln1_g[l], ln1_b[l], w_up[l], w_down[l], ln2_g[l], ln2_b[l], alpha)
    return x2d.reshape(B, S, D)
```
